```python
import math
import jax, jax.numpy as jnp
from jax import lax
import numpy as np

D_MODEL = 4096
BATCH = 2
SEQ = 8192
DEPTH = 1

GRID_W = 64
CTX_LEN = 256

NA_HEAD_DIM = 128
NA_WIDTH = D_MODEL // 2
NA_HEADS = NA_WIDTH // NA_HEAD_DIM
NA_KH = 8
NA_KW = 16

HG_HEADS = 16
HG_KDIM = 128
HG_VDIM = (D_MODEL // 2) // HG_HEADS
HG_WIDTH = HG_HEADS * HG_KDIM
HG_CHUNK = 64

ALPHA = (2.0 * DEPTH) ** 0.25
BETA = (8.0 * DEPTH) ** -0.25
LN_EPS = 1e-6
RMS_EPS = 1e-6

kernel_name = 'hybrid_na_hgrn2_dit_block'


def _in_sizes():
    return [NA_WIDTH] * 4 + [HG_WIDTH] * 5 + [D_MODEL] * 2


def _split_points():
    return [int(p) for p in np.cumsum(_in_sizes())[:-1]]


def layer_norm(x):
    x32 = x.astype(jnp.float32)
    mu = jnp.mean(x32, axis=-1, keepdims=True)
    var = jnp.mean(jnp.square(x32 - mu), axis=-1, keepdims=True)
    return ((x32 - mu) * lax.rsqrt(var + LN_EPS)).astype(x.dtype)


def adaln_params(cond, w_ada, b_ada):
    mod = jax.nn.silu(cond) @ w_ada + b_ada
    return jnp.split(mod, 3, axis=-1)


def split_heads(t, n_heads):
    b, n, w = t.shape
    return t.reshape(b, n, n_heads, w // n_heads).transpose(0, 2, 1, 3)


def merge_heads(t):
    b, h, n, d = t.shape
    return t.transpose(0, 2, 1, 3).reshape(b, n, h * d)


def neighbourhood_attention(q, k, v, k_ctx, v_ctx, rpb):
    b, h, t, dh = q.shape
    rows = t // GRID_W
    kh = min(NA_KH, rows)
    qg = q.reshape(b, h, rows, GRID_W, dh) * (dh ** -0.5)
    kg = k.reshape(b, h, rows, GRID_W, dh)
    vg = v.reshape(b, h, rows, GRID_W, dh)
    r = jnp.arange(rows)
    row_start = jnp.clip(r - NA_KH // 2, 0, rows - kh)
    row_idx = row_start[:, None] + jnp.arange(kh)[None, :]
    k_strip = kg[:, :, row_idx].reshape(b, h, rows, kh * GRID_W, dh)
    v_strip = vg[:, :, row_idx].reshape(b, h, rows, kh * GRID_W, dh)
    col = jnp.arange(GRID_W)
    col_start = jnp.clip(col - NA_KW // 2, 0, GRID_W - NA_KW)
    col_in = (col[None, :] >= col_start[:, None]) & (col[None, :] < col_start[:, None] + NA_KW)
    d_row = row_idx - r[:, None]
    d_col = jnp.clip(col[None, :] - col[:, None], -(NA_KW - 1), NA_KW - 1)
    bias = rpb.astype(jnp.float32)[:, d_row[:, None, :, None] + NA_KH - 1,
                                   d_col[None, :, None, :] + NA_KW - 1]
    bias = jnp.where(col_in[:, None, :], bias, -jnp.inf).reshape(h, rows, GRID_W, kh * GRID_W)
    s_loc = jnp.einsum('bhrqd,bhrkd->bhrqk', qg, k_strip).astype(jnp.float32) + bias[None]
    s_ctx = jnp.einsum('bhrqd,bhld->bhrql', qg, k_ctx).astype(jnp.float32)
    p = jax.nn.softmax(jnp.concatenate([s_loc, s_ctx], axis=-1), axis=-1).astype(v.dtype)
    n_loc = kh * GRID_W
    o = (jnp.einsum('bhrqk,bhrkd->bhrqd', p[..., :n_loc], v_strip)
         + jnp.einsum('bhrql,bhld->bhrqd', p[..., n_loc:], v_ctx))
    return o.reshape(b, h, t, dh)


def context_attention(q, k, v):
    s = jnp.einsum('bhqd,bhkd->bhqk', q, k).astype(jnp.float32) * (q.shape[-1] ** -0.5)
    p = jax.nn.softmax(s, axis=-1).astype(v.dtype)
    return jnp.einsum('bhqk,bhkd->bhqd', p, v)


def hgrn2_lower_bound(lb_logits, layer):
    lb = jnp.cumsum(jax.nn.softmax(lb_logits.astype(jnp.float32), axis=0), axis=0)[layer]
    return lb.reshape(HG_HEADS, 1, HG_KDIM)


def hgrn2_forget(f_pre, lb):
    f_pre = f_pre.astype(jnp.float32)
    f = lb + (1.0 - lb) * jax.nn.sigmoid(f_pre)
    return (1.0 - lb) * jax.nn.sigmoid(-f_pre), jnp.log(f)


def gla_chunked(q, k, v, log_f, s0):
    b, h, t, dk = q.shape
    dv = v.shape[-1]
    n, c = t // HG_CHUNK, HG_CHUNK
    qc = q.reshape(b, h, n, c, dk)
    kc = k.reshape(b, h, n, c, dk)
    vc = v.reshape(b, h, n, c, dv)
    cum = jnp.cumsum(log_f.reshape(b, h, n, c, dk), axis=3)
    ref = cum[:, :, :, c // 2 - 1:c // 2]
    a = jnp.einsum('bhncd,bhnsd->bhncs', qc * jnp.exp(cum - ref), kc * jnp.exp(ref - cum))
    a = jnp.where(jnp.tril(jnp.ones((c, c), dtype=bool)), a, 0.0)
    o_intra = jnp.einsum('bhncs,bhnsv->bhncv', a, vc)
    last = cum[:, :, :, -1]
    u = jnp.einsum('bhncd,bhncv->bhndv', kc * jnp.exp(last[:, :, :, None] - cum), vc)

    def step(s, inp):
        decay, du = inp
        return decay[..., None] * s + du, s

    s_last, s_start = lax.scan(step, s0, (jnp.moveaxis(jnp.exp(last), 2, 0), jnp.moveaxis(u, 2, 0)))
    s_start = jnp.moveaxis(s_start, 0, 2)
    o_inter = jnp.einsum('bhncd,bhndv->bhncv', qc * jnp.exp(cum), s_start)
    return (o_intra + o_inter).reshape(b, h, t, dv), s_last


def hgrn2_readout(o, g, norm_w):
    o = o * lax.rsqrt(jnp.mean(jnp.square(o), axis=-1, keepdims=True) + RMS_EPS) * norm_w
    return merge_heads(o.astype(g.dtype)) * jax.nn.silu(g)


def merge_branches(y_a, y_b, gate_a, gate_b, w_pa, w_pb, w_out):
    m = jax.nn.sigmoid(gate_a) * (y_a @ w_pa) + jax.nn.sigmoid(gate_b) * (y_b @ w_pb)
    return m @ w_out


def trunk_layer(x, ctx, c, c_ctx, w_ada, b_ada, w_in, na_rpb, lb_fwd, lb_bwd, hg_norm_w,
                w_pa, w_pb, w_out, ln_g, ln_b, layer, update_ctx):
    bsz = x.shape[0]
    L = ctx.shape[1]
    shift, scale, gate = adaln_params(c, w_ada, b_ada)
    shift_c, scale_c, gate_c = adaln_params(c_ctx, w_ada, b_ada)
    h_lat = layer_norm(x) * (1.0 + scale[:, None]) + shift[:, None]
    h_ctx = layer_norm(ctx) * (1.0 + scale_c) + shift_c
    proj = jnp.concatenate([h_ctx, h_lat], axis=1) @ w_in
    (na_q, na_k, na_v, na_z, hg_q, hg_ff, hg_fb, hg_i, hg_g,
     gate_a, gate_b) = jnp.split(proj, _split_points(), axis=-1)

    q, k, v = (split_heads(t, NA_HEADS) for t in (na_q, na_k, na_v))
    o_a = neighbourhood_attention(q[:, :, L:], k[:, :, L:], v[:, :, L:], k[:, :, :L], v[:, :, :L], na_rpb)
    y_a = merge_heads(o_a) * jax.nn.silu(na_z[:, L:])

    hq = split_heads(jax.nn.silu(hg_q), HG_HEADS).astype(jnp.float32)
    hv = split_heads(hg_i, HG_HEADS).astype(jnp.float32)
    k_f, logf_f = hgrn2_forget(split_heads(hg_ff, HG_HEADS), hgrn2_lower_bound(lb_fwd, layer))
    k_b, logf_b = hgrn2_forget(split_heads(hg_fb, HG_HEADS), hgrn2_lower_bound(lb_bwd, layer))
    rev = lambda t: jnp.flip(t, axis=2)
    s0 = jnp.zeros((bsz, HG_HEADS, HG_KDIM, HG_VDIM), jnp.float32)
    o_cf, s_cf = gla_chunked(hq[:, :, :L], k_f[:, :, :L], hv[:, :, :L], logf_f[:, :, :L], s0)
    o_cb, s_cb = gla_chunked(rev(hq[:, :, :L]), rev(k_b[:, :, :L]), rev(hv[:, :, :L]),
                             rev(logf_b[:, :, :L]), s0)
    o_lf, _ = gla_chunked(hq[:, :, L:], k_f[:, :, L:], hv[:, :, L:], logf_f[:, :, L:], s_cf)
    o_lb, _ = gla_chunked(rev(hq[:, :, L:]), rev(k_b[:, :, L:]), rev(hv[:, :, L:]),
                          rev(logf_b[:, :, L:]), s_cb)
    y_b = hgrn2_readout(o_lf + rev(o_lb), hg_g[:, L:], hg_norm_w)

    out = merge_branches(y_a, y_b, gate_a[:, L:], gate_b[:, L:], w_pa, w_pb, w_out)
    x_new = layer_norm(ALPHA * x + gate[:, None] * out) * ln_g + ln_b
    if not update_ctx:
        return x_new, ctx

    y_ac = merge_heads(context_attention(q[:, :, :L], k[:, :, :L], v[:, :, :L])) * jax.nn.silu(na_z[:, :L])
    y_bc = hgrn2_readout(o_cf + rev(o_cb), hg_g[:, :L], hg_norm_w)
    out_c = merge_branches(y_ac, y_bc, gate_a[:, :L], gate_b[:, :L], w_pa, w_pb, w_out)
    ctx_new = layer_norm(ALPHA * ctx + gate_c * out_c) * ln_g + ln_b
    return x_new, ctx_new


def setup_inputs(seed: int = 0) -> dict:
    key = jax.random.key(seed)
    ks = jax.random.split(key, 16)
    d = D_MODEL
    nrm = jax.random.normal
    sizes = _in_sizes()
    n_in = sum(sizes)
    col_scale = jnp.concatenate([jnp.full((s,), BETA if j in (2, 7) else 1.0, jnp.float32)
                                 for j, s in enumerate(sizes)])
    return {
        'x': nrm(ks[0], (BATCH, SEQ, d), jnp.float32),
        'c': nrm(ks[1], (BATCH, d), jnp.float32),
        'ctx': nrm(ks[2], (BATCH, CTX_LEN, d), jnp.float32),
        'c_ctx': nrm(ks[3], (d,), jnp.float32),
        'w_ada': nrm(ks[4], (DEPTH, d, 3 * d), jnp.float32) * (0.5 * d ** -0.5),
        'b_ada': nrm(ks[5], (DEPTH, 3 * d), jnp.float32) * 0.01,
        'w_in': nrm(ks[6], (DEPTH, d, n_in), jnp.float32) * (d ** -0.5) * col_scale,
        'na_rpb': nrm(ks[7], (DEPTH, NA_HEADS, 2 * NA_KH - 1, 2 * NA_KW - 1), jnp.float32) * 0.02,
        'hg_lb_fwd': nrm(ks[8], (DEPTH + 1, HG_WIDTH), jnp.float32) * 0.1,
        'hg_lb_bwd': nrm(ks[9], (DEPTH + 1, HG_WIDTH), jnp.float32) * 0.1,
        'hg_norm_w': 1.0 + 0.01 * nrm(ks[10], (DEPTH, HG_VDIM), jnp.float32),
        'w_pa': nrm(ks[11], (DEPTH, NA_WIDTH, d), jnp.float32) * (NA_WIDTH ** -0.5) * BETA,
        'w_pb': nrm(ks[12], (DEPTH, HG_WIDTH, d), jnp.float32) * (HG_WIDTH ** -0.5) * BETA,
        'w_out': nrm(ks[13], (DEPTH, d, d), jnp.float32) * (d ** -0.5) * BETA,
        'ln_g': 1.0 + 0.01 * nrm(ks[14], (DEPTH, d), jnp.float32),
        'ln_b': 0.01 * nrm(ks[15], (DEPTH, d), jnp.float32),
    }


def reference(x, c, ctx, c_ctx, w_ada, b_ada, w_in, na_rpb, hg_lb_fwd, hg_lb_bwd, hg_norm_w,
              w_pa, w_pb, w_out, ln_g, ln_b):
    for l in range(DEPTH):
        x, ctx = trunk_layer(x, ctx, c, c_ctx, w_ada[l], b_ada[l], w_in[l], na_rpb[l],
                             hg_lb_fwd, hg_lb_bwd, hg_norm_w[l], w_pa[l], w_pb[l], w_out[l],
                             ln_g[l], ln_b[l], l, l < DEPTH - 1)
    return x
```

```python
import functools
import math

import jax
import jax.numpy as jnp
from jax import lax
from jax.experimental import pallas as pl
from jax.experimental.pallas import tpu as pltpu

F32 = jnp.float32
BF16 = jnp.bfloat16

GRID_W = 64
HEAD = 128
NA_KH = 8
NA_KW = 16
NA_QROWS = 4
NA_KROWS = 12
CHUNK = 64
LN_EPS = 1e-6
RMS_EPS = 1e-6
VMEM_LIMIT = 56 * 1024 * 1024


def _params(semantics, vmem=VMEM_LIMIT):
    return pltpu.CompilerParams(dimension_semantics=semantics, vmem_limit_bytes=vmem)


def _tile(n, pref):
    t = min(n, pref)
    assert n % t == 0, (n, t)
    return t


def _adaln_kernel(c_ref, w_ref, b_ref, o_ref):
    c = c_ref[...]
    s = c * jax.nn.sigmoid(c)
    s_hi = s.astype(BF16)
    s_lo = (s - s_hi.astype(F32)).astype(BF16)
    w = w_ref[...]
    w_hi = w.astype(BF16)
    w_lo = (w - w_hi.astype(F32)).astype(BF16)
    n = s.shape[0]
    lhs = jnp.concatenate([s_hi, s_lo], axis=0)
    a = jnp.dot(lhs, w_hi, preferred_element_type=F32)
    b = jnp.dot(s_hi, w_lo, preferred_element_type=F32)
    o_ref[...] = a[:n] + a[n:] + b + b_ref[...]


def _adaln(cond, w, b):
    d, n = w.shape
    tn = _tile(n, 512)
    return pl.pallas_call(
        _adaln_kernel,
        out_shape=jax.ShapeDtypeStruct((cond.shape[0], n), F32),
        grid=(n // tn,),
        in_specs=[pl.BlockSpec(cond.shape, lambda j: (0, 0)),
                  pl.BlockSpec((d, tn), lambda j: (0, j)),
                  pl.BlockSpec((1, tn), lambda j: (0, j))],
        out_specs=pl.BlockSpec((cond.shape[0], tn), lambda j: (0, j)),
        compiler_params=_params(("arbitrary",)),
        name="adaln",
    )(cond, w, b)


def _ln_mod_kernel(x_ref, shift_ref, scale_ref, o_ref):
    x = x_ref[0]
    mu = jnp.mean(x, axis=-1, keepdims=True)
    xc = x - mu
    var = jnp.mean(xc * xc, axis=-1, keepdims=True)
    y = xc * lax.rsqrt(var + LN_EPS)
    o_ref[0] = (y * (1.0 + scale_ref[0]) + shift_ref[0]).astype(BF16)


def _ln_mod(x, shift, scale, cond_row):
    bsz, t, d = x.shape
    tm = _tile(t, 256)
    cmap = lambda b, i: (cond_row(b), 0, 0)
    return pl.pallas_call(
        _ln_mod_kernel,
        out_shape=jax.ShapeDtypeStruct((bsz, t, d), BF16),
        grid=(bsz, t // tm),
        in_specs=[pl.BlockSpec((1, tm, d), lambda b, i: (b, i, 0)),
                  pl.BlockSpec((1, 1, d), cmap),
                  pl.BlockSpec((1, 1, d), cmap)],
        out_specs=pl.BlockSpec((1, tm, d), lambda b, i: (b, i, 0)),
        compiler_params=_params(("arbitrary", "arbitrary")),
        name="ln_mod",
    )(x, shift, scale)


def _proj_kernel(h_ref, w_ref, o_ref):
    acc = jnp.dot(h_ref[0], w_ref[...], preferred_element_type=F32)
    for s in range(o_ref.shape[1]):
        o_ref[0, s] = acc[:, s * HEAD:(s + 1) * HEAD].astype(BF16)


def _proj(h, w, n_tiles, col_tile, tn, name):
    bsz, t, d = h.shape
    tm = _tile(t, 1024)
    sub = tn // HEAD
    return pl.pallas_call(
        _proj_kernel,
        out_shape=jax.ShapeDtypeStruct((bsz, n_tiles * sub, t, HEAD), BF16),
        grid=(bsz, t // tm, n_tiles),
        in_specs=[pl.BlockSpec((1, tm, d), lambda b, i, j: (b, i, 0)),
                  pl.BlockSpec((d, tn), lambda b, i, j: (0, col_tile(j)))],
        out_specs=pl.BlockSpec((1, sub, tm, HEAD), lambda b, i, j: (b, j, i, 0)),
        compiler_params=_params(("arbitrary", "arbitrary", "arbitrary")),
        name=name,
    )(h, w)


def _na_bias(rpb, rows):
    nh = rpb.shape[0]
    col = jnp.arange(GRID_W)
    col_start = jnp.clip(col - NA_KW // 2, 0, GRID_W - NA_KW)
    col_in = (col[None, :] >= col_start[:, None]) & (col[None, :] < col_start[:, None] + NA_KW)
    d_col = jnp.clip(col[None, :] - col[:, None], -(NA_KW - 1), NA_KW - 1)
    out = []
    for r0, kb in ((0, 0), (NA_QROWS, 0), (rows - NA_QROWS, rows - NA_KROWS)):
        qr = r0 + jnp.arange(NA_QROWS)
        kr = kb + jnp.arange(NA_KROWS)
        row_start = jnp.clip(qr - NA_KH // 2, 0, rows - NA_KH)
        row_in = (kr[None, :] >= row_start[:, None]) & (kr[None, :] < row_start[:, None] + NA_KH)
        d_row = jnp.clip(kr[None, :] - qr[:, None], -(NA_KH - 1), NA_KH - 1)
        b = rpb.astype(F32)[:, d_row[:, None, :, None] + NA_KH - 1, d_col[None, :, None, :] + NA_KW - 1]
        mask = row_in[:, None, :, None] & col_in[None, :, None, :]
        b = jnp.where(mask[None], b, -jnp.inf)
        out.append(b.reshape(nh, NA_QROWS * GRID_W, NA_KROWS * GRID_W))
    return jnp.stack(out, axis=1)


def _na_kernel(q_ref, k_ref, v_ref, z_ref, kc_ref, vc_ref, bias_ref, o_ref, *, rows):
    nq = NA_QROWS * GRID_W
    nk = NA_KROWS * GRID_W
    n_blocks = rows // NA_QROWS
    scale = HEAD ** -0.5
    kc = kc_ref[0, 0]
    vc = vc_ref[0, 0]
    nt = (((1,), (1,)), ((), ()))

    def block(i, carry):
        r0 = i * NA_QROWS
        kb = jnp.clip(r0 - NA_KH // 2, 0, rows - NA_KROWS)
        kind = jnp.where(i == 0, 0, jnp.where(i == n_blocks - 1, 2, 1))
        q_off = pl.multiple_of(r0 * GRID_W, GRID_W)
        k_off = pl.multiple_of(kb * GRID_W, GRID_W)
        q = (q_ref[0, 0, pl.ds(q_off, nq), :].astype(F32) * scale).astype(BF16)
        ks = k_ref[0, 0, pl.ds(k_off, nk), :]
        vs = v_ref[0, 0, pl.ds(k_off, nk), :]
        s_loc = lax.dot_general(q, ks, nt, preferred_element_type=F32) + bias_ref[0, kind]
        s_ctx = lax.dot_general(q, kc, nt, preferred_element_type=F32)
        m = jnp.maximum(jnp.max(s_loc, axis=-1, keepdims=True), jnp.max(s_ctx, axis=-1, keepdims=True))
        p_loc = jnp.exp(s_loc - m)
        p_ctx = jnp.exp(s_ctx - m)
        denom = jnp.sum(p_loc, axis=-1, keepdims=True) + jnp.sum(p_ctx, axis=-1, keepdims=True)
        o = (jnp.dot(p_loc.astype(BF16), vs, preferred_element_type=F32)
             + jnp.dot(p_ctx.astype(BF16), vc, preferred_element_type=F32))
        o = o / denom
        z = z_ref[0, 0, pl.ds(q_off, nq), :].astype(F32)
        o_ref[0, 0, pl.ds(q_off, nq), :] = (o * (z * jax.nn.sigmoid(z))).astype(BF16)
        return carry

    lax.fori_loop(0, n_blocks, block, 0)


def _na(proj, proj_ctx, bias, nh):
    bsz, _, t, _ = proj.shape
    lc = proj_ctx.shape[2]
    rows = t // GRID_W
    assert rows % NA_QROWS == 0 and rows >= NA_KROWS + NA_QROWS
    lat = lambda g: pl.BlockSpec((1, 1, t, HEAD), lambda b, h: (b, g * nh + h, 0, 0))
    ctx = lambda g: pl.BlockSpec((1, 1, lc, HEAD), lambda b, h: (b, g * nh + h, 0, 0))
    return pl.pallas_call(
        functools.partial(_na_kernel, rows=rows),
        out_shape=jax.ShapeDtypeStruct((bsz, nh, t, HEAD), BF16),
        grid=(bsz, nh),
        in_specs=[lat(0), lat(1), lat(2), lat(3), ctx(0), ctx(1),
                  pl.BlockSpec((1,) + bias.shape[1:], lambda b, h: (h, 0, 0, 0))],
        out_specs=pl.BlockSpec((1, 1, t, HEAD), lambda b, h: (b, h, 0, 0)),
        compiler_params=_params(("arbitrary", "arbitrary")),
        name="na",
    )(proj, proj, proj, proj, proj_ctx, proj_ctx, bias)


def _chunk_cumsum(x, reverse):
    row = lax.broadcasted_iota(jnp.int32, x.shape, 0)
    s = 1
    while s < CHUNK:
        if reverse:
            x = x + jnp.where(row < CHUNK - s, pltpu.roll(x, CHUNK - s, 0), 0.0)
        else:
            x = x + jnp.where(row >= s, pltpu.roll(x, s, 0), 0.0)
        s *= 2
    return x


def _forget(f_pre, lb):
    sg = jax.nn.sigmoid(f_pre)
    one_m_lb = 1.0 - lb
    return one_m_lb * (1.0 - sg), jnp.log(lb + one_m_lb * sg)


def _lower_bound(logits):
    m = jnp.max(logits, axis=0, keepdims=True)
    e = jnp.exp(logits - m)
    return e[0:1] / jnp.sum(e, axis=0, keepdims=True)


def _state_chunk(st, f_pre, v, lb, reverse):
    kk, logf = _forget(f_pre, lb)
    cum = _chunk_cumsum(logf, reverse)
    last = cum[0:1] if reverse else cum[CHUNK - 1:CHUNK]
    kl = (kk * jnp.exp(last - cum)).astype(BF16)
    tn = (((0,), (0,)), ((), ()))
    u_t = lax.dot_general(v, kl, tn, preferred_element_type=F32)
    return st * jnp.exp(last) + u_t


def _full_chunk(st, q_pre, f_pre, v, lb, reverse, tri):
    kk, logf = _forget(f_pre, lb)
    cum = _chunk_cumsum(logf, reverse)
    mid = CHUNK // 2
    ref = cum[mid:mid + 1] if reverse else cum[mid - 1:mid]
    last = cum[0:1] if reverse else cum[CHUNK - 1:CHUNK]
    qs = q_pre * jax.nn.sigmoid(q_pre)
    qd = qs * jnp.exp(cum - ref)
    kd = kk * jnp.exp(ref - cum)
    nt = (((1,), (1,)), ((), ()))
    tn = (((0,), (0,)), ((), ()))
    a = lax.dot_general(qd.astype(BF16), kd.astype(BF16), nt, preferred_element_type=F32)
    a = jnp.where(tri, a, 0.0)
    o = jnp.dot(a.astype(BF16), v, preferred_element_type=F32)
    qe = (qd * jnp.exp(ref)).astype(BF16)
    o = o + lax.dot_general(qe, st.astype(BF16), nt, preferred_element_type=F32)
    kl = (kd * jnp.exp(last - ref)).astype(BF16)
    u_t = lax.dot_general(v, kl, tn, preferred_element_type=F32)
    return o, st * jnp.exp(last) + u_t


def _hg_kernel(q_ref, ff_ref, fb_ref, i_ref, g_ref, ffc_ref, fbc_ref, ic_ref,
               lbf_ref, lbb_ref, nw_ref, o_ref, ob_ref, *, t, lc, unroll):
    n_lat = t // CHUNK
    n_ctx = lc // CHUNK
    lb_f = _lower_bound(lbf_ref[...])
    lb_b = _lower_bound(lbb_ref[...])
    nw = nw_ref[...]
    r_i = lax.broadcasted_iota(jnp.int32, (CHUNK, CHUNK), 0)
    c_i = lax.broadcasted_iota(jnp.int32, (CHUNK, CHUNK), 1)
    tri_f = r_i >= c_i
    tri_b = r_i <= c_i

    def sl(ref, c):
        return ref[0, 0, pl.ds(pl.multiple_of(c * CHUNK, CHUNK), CHUNK), :]

    st0 = jnp.zeros((HEAD, HEAD), F32)

    def ctx_b(n, st):
        c = n_ctx - 1 - n
        return _state_chunk(st, sl(fbc_ref, c).astype(F32), sl(ic_ref, c), lb_b, True)

    st = lax.fori_loop(0, n_ctx, ctx_b, st0)

    def lat_b(n, st):
        for u in range(unroll):
            c = n_lat - 1 - (n * unroll + u)
            o, st = _full_chunk(st, sl(q_ref, c).astype(F32), sl(fb_ref, c).astype(F32), sl(i_ref, c),
                                lb_b, True, tri_b)
            ob_ref[pl.ds(pl.multiple_of(c * CHUNK, CHUNK), CHUNK), :] = o
        return st

    lax.fori_loop(0, n_lat // unroll, lat_b, st)

    def ctx_f(c, st):
        return _state_chunk(st, sl(ffc_ref, c).astype(F32), sl(ic_ref, c), lb_f, False)

    st = lax.fori_loop(0, n_ctx, ctx_f, st0)

    def lat_f(n, st):
        for u in range(unroll):
            c = n * unroll + u
            off = pl.multiple_of(c * CHUNK, CHUNK)
            o, st = _full_chunk(st, sl(q_ref, c).astype(F32), sl(ff_ref, c).astype(F32), sl(i_ref, c),
                                lb_f, False, tri_f)
            o = o + ob_ref[pl.ds(off, CHUNK), :]
            o = o * lax.rsqrt(jnp.mean(o * o, axis=-1, keepdims=True) + RMS_EPS) * nw
            g = sl(g_ref, c).astype(F32)
            o_ref[0, 0, pl.ds(off, CHUNK), :] = (o * (g * jax.nn.sigmoid(g))).astype(BF16)
        return st

    lax.fori_loop(0, n_lat // unroll, lat_f, st)


def _hgrn2(proj, proj_ctx, lb_f, lb_b, norm_w, nh):
    bsz, _, t, _ = proj.shape
    lc = proj_ctx.shape[2]
    n_slots = lb_f.shape[0]
    unroll = 2
    assert t % (CHUNK * unroll) == 0 and lc % CHUNK == 0
    lat = lambda g: pl.BlockSpec((1, 1, t, HEAD), lambda b, h: (b, g * nh + h, 0, 0))
    ctx = lambda g: pl.BlockSpec((1, 1, lc, HEAD), lambda b, h: (b, g * nh + h, 0, 0))
    lbs = pl.BlockSpec((n_slots, HEAD), lambda b, h: (0, h))
    return pl.pallas_call(
        functools.partial(_hg_kernel, t=t, lc=lc, unroll=unroll),
        out_shape=jax.ShapeDtypeStruct((bsz, nh, t, HEAD), BF16),
        grid=(bsz, nh),
        in_specs=[lat(4), lat(5), lat(6), lat(7), lat(8), ctx(2), ctx(3), ctx(4), lbs, lbs,
                  pl.BlockSpec((1, HEAD), lambda b, h: (0, 0))],
        out_specs=pl.BlockSpec((1, 1, t, HEAD), lambda b, h: (b, h, 0, 0)),
        scratch_shapes=[pltpu.VMEM((t, HEAD), F32)],
        compiler_params=_params(("arbitrary", "arbitrary")),
        name="hgrn2",
    )(proj, proj, proj, proj, proj, proj_ctx, proj_ctx, proj_ctx, lb_f, lb_b, norm_w)


def _merge_kernel(ya_ref, yb_ref, wa_ref, wb_ref, ga_ref, gb_ref, o_ref, ya_s, yb_s):
    @pl.when(pl.program_id(2) == 0)
    def _():
        for h in range(ya_ref.shape[1]):
            ya_s[:, h * HEAD:(h + 1) * HEAD] = ya_ref[0, h]
            yb_s[:, h * HEAD:(h + 1) * HEAD] = yb_ref[0, h]

    ta = jnp.dot(ya_s[...], wa_ref[...], preferred_element_type=F32)
    tb = jnp.dot(yb_s[...], wb_ref[...], preferred_element_type=F32)
    for s in range(ga_ref.shape[1]):
        cs = slice(s * HEAD, (s + 1) * HEAD)
        m = (jax.nn.sigmoid(ga_ref[0, s].astype(F32)) * ta[:, cs]
             + jax.nn.sigmoid(gb_ref[0, s].astype(F32)) * tb[:, cs])
        o_ref[0, :, cs] = m.astype(BF16)


def _merge(ya, yb, w_pa, w_pb, proj, nh):
    bsz, _, t, _ = ya.shape
    d = w_pa.shape[1]
    tm = _tile(t, 512)
    tn = _tile(nh * HEAD, 1024)
    sub = tn // HEAD
    ga0 = 9 * nh // sub
    gb0 = 11 * nh // sub
    assert (9 * nh) % sub == 0 and (11 * nh) % sub == 0
    yspec = pl.BlockSpec((1, nh, tm, HEAD), lambda b, i, j: (b, 0, i, 0))
    wspec = pl.BlockSpec((nh * HEAD, tn), lambda b, i, j: (0, j))
    return pl.pallas_call(
        _merge_kernel,
        out_shape=jax.ShapeDtypeStruct((bsz, t, d), BF16),
        grid=(bsz, t // tm, d // tn),
        in_specs=[yspec, yspec, wspec, wspec,
                  pl.BlockSpec((1, sub, tm, HEAD), lambda b, i, j: (b, ga0 + j, i, 0)),
                  pl.BlockSpec((1, sub, tm, HEAD), lambda b, i, j: (b, gb0 + j, i, 0))],
        out_specs=pl.BlockSpec((1, tm, tn), lambda b, i, j: (b, i, j)),
        scratch_shapes=[pltpu.VMEM((tm, nh * HEAD), BF16), pltpu.VMEM((tm, nh * HEAD), BF16)],
        compiler_params=_params(("arbitrary", "arbitrary", "arbitrary")),
        name="merge",
    )(ya, yb, w_pa, w_pb, proj, proj)


def _out_kernel(m_ref, w_ref, x_ref, gate_ref, lng_ref, lnb_ref, o_ref, *, alpha, tn):
    j = pl.program_id(2)
    out = jnp.dot(m_ref[0], w_ref[...], preferred_element_type=F32)
    col = pl.multiple_of(j * tn, tn)
    o_ref[0, :, pl.ds(col, tn)] = alpha * x_ref[0] + gate_ref[0] * out

    @pl.when(j == pl.num_programs(2) - 1)
    def _():
        r = o_ref[0]
        mu = jnp.mean(r, axis=-1, keepdims=True)
        rc = r - mu
        var = jnp.mean(rc * rc, axis=-1, keepdims=True)
        o_ref[0] = rc * lax.rsqrt(var + LN_EPS) * lng_ref[...] + lnb_ref[...]


def _out(m, w_out, x, gate, ln_g, ln_b, alpha):
    bsz, t, d = x.shape
    tm = _tile(t, 256)
    tn = _tile(d, 1024)
    return pl.pallas_call(
        functools.partial(_out_kernel, alpha=alpha, tn=tn),
        out_shape=jax.ShapeDtypeStruct((bsz, t, d), F32),
        grid=(bsz, t // tm, d // tn),
        in_specs=[pl.BlockSpec((1, tm, d), lambda b, i, j: (b, i, 0)),
                  pl.BlockSpec((d, tn), lambda b, i, j: (0, j)),
                  pl.BlockSpec((1, tm, tn), lambda b, i, j: (b, i, j)),
                  pl.BlockSpec((1, 1, tn), lambda b, i, j: (b, 0, j)),
                  pl.BlockSpec((1, d), lambda b, i, j: (0, 0)),
                  pl.BlockSpec((1, d), lambda b, i, j: (0, 0))],
        out_specs=pl.BlockSpec((1, tm, d), lambda b, i, j: (b, i, 0)),
        compiler_params=_params(("arbitrary", "arbitrary", "arbitrary")),
        name="out",
    )(m, w_out, x, gate, ln_g, ln_b)


def _layer(x, ctx, cond, w_ada, b_ada, w_in, rpb, lb_f, lb_b, norm_w, w_pa, w_pb, w_out, ln_g, ln_b, alpha):
    bsz, t, d = x.shape
    nh = (d // 2) // HEAD
    grp = d // 2
    tn = _tile(grp, 1024)
    per = grp // tn

    mod = _adaln(cond, w_ada, b_ada.reshape(1, -1))
    shift, scale, gate = (mod[:3, k * d:(k + 1) * d].reshape(3, 1, d) for k in range(3))
    ctx_row = bsz

    h_lat = _ln_mod(x, shift, scale, lambda b: b)
    h_ctx = _ln_mod(ctx, shift, scale, lambda b: ctx_row)

    w_in16 = w_in.astype(BF16)
    proj = _proj(h_lat, w_in16, 13 * per, lambda j: j, tn, "proj_lat")
    proj_ctx = _proj(h_ctx, w_in16, 5 * per, lambda j: jnp.where(j < 2 * per, j + per, j + 3 * per), tn,
                     "proj_ctx")

    rows = t // GRID_W
    y_a = _na(proj, proj_ctx, _na_bias(rpb, rows), nh)
    y_b = _hgrn2(proj, proj_ctx, lb_f, lb_b, norm_w.reshape(1, HEAD), nh)
    m = _merge(y_a, y_b, w_pa.astype(BF16), w_pb.astype(BF16), proj, nh)
    return _out(m, w_out.astype(BF16), x, gate, ln_g.reshape(1, d), ln_b.reshape(1, d), alpha)


def kernel(x, c, ctx, c_ctx, w_ada, b_ada, w_in, na_rpb, hg_lb_fwd, hg_lb_bwd, hg_norm_w, w_pa, w_pb, w_out,
           ln_g, ln_b):
    depth = w_ada.shape[0]
    assert depth == 1, "the context stream update of deeper stacks is not implemented"
    bsz, _, d = x.shape
    alpha = (2.0 * depth) ** 0.25
    cond = jnp.concatenate([c, c_ctx[None], jnp.zeros((8 - bsz - 1, d), c.dtype)], axis=0)
    return _layer(x, ctx, cond, w_ada[0], b_ada[0], w_in[0], na_rpb[0], hg_lb_fwd, hg_lb_bwd, hg_norm_w[0],
                  w_pa[0], w_pb[0], w_out[0], ln_g[0], ln_b[0], alpha)
```

```python
import functools
import math

import jax
import jax.numpy as jnp
from jax import lax
from jax.experimental import pallas as pl
from jax.experimental.pallas import tpu as pltpu

F32 = jnp.float32
BF16 = jnp.bfloat16

GRID_W = 64
HEAD = 128
NA_KH = 8
NA_KW = 16
NA_QROWS = 4
NA_KROWS = 12
NA_UNROLL = 2
CHUNK = 64
LN_EPS = 1e-6
RMS_EPS = 1e-6
LN_ROWS = 64
VMEM_LIMIT = 56 * 1024 * 1024


def _params(semantics, vmem=VMEM_LIMIT):
    return pltpu.CompilerParams(dimension_semantics=semantics, vmem_limit_bytes=vmem)


def _tile(n, pref):
    t = min(n, pref)
    assert n % t == 0, (n, t)
    return t


def _adaln_kernel(c_ref, w_ref, b_ref, o_ref):
    c = c_ref[...]
    s = c * jax.nn.sigmoid(c)
    s_hi = s.astype(BF16)
    s_lo = (s - s_hi.astype(F32)).astype(BF16)
    w = w_ref[...]
    w_hi = w.astype(BF16)
    w_lo = (w - w_hi.astype(F32)).astype(BF16)
    n = s.shape[0]
    lhs = jnp.concatenate([s_hi, s_lo], axis=0)
    a = jnp.dot(lhs, w_hi, preferred_element_type=F32)
    b = jnp.dot(s_hi, w_lo, preferred_element_type=F32)
    o_ref[...] = a[:n] + a[n:] + b + b_ref[...]


def _adaln(cond, w, b):
    d, n = w.shape
    tn = _tile(n, 512)
    return pl.pallas_call(
        _adaln_kernel,
        out_shape=jax.ShapeDtypeStruct((cond.shape[0], n), F32),
        grid=(n // tn,),
        in_specs=[pl.BlockSpec(cond.shape, lambda j: (0, 0)),
                  pl.BlockSpec((d, tn), lambda j: (0, j)),
                  pl.BlockSpec((1, tn), lambda j: (0, j))],
        out_specs=pl.BlockSpec((cond.shape[0], tn), lambda j: (0, j)),
        compiler_params=_params(("arbitrary",)),
        name="adaln",
    )(cond, w, b)


def _ln_mod_kernel(x_ref, shift_ref, scale_ref, o_ref):
    x = x_ref[0]
    mu = jnp.mean(x, axis=-1, keepdims=True)
    xc = x - mu
    var = jnp.mean(xc * xc, axis=-1, keepdims=True)
    y = xc * lax.rsqrt(var + LN_EPS)
    o_ref[0] = (y * (1.0 + scale_ref[0]) + shift_ref[0]).astype(BF16)


def _ln_mod(x, shift, scale, cond_row):
    bsz, t, d = x.shape
    tm = _tile(t, 256)
    cmap = lambda b, i: (cond_row(b), 0, 0)
    return pl.pallas_call(
        _ln_mod_kernel,
        out_shape=jax.ShapeDtypeStruct((bsz, t, d), BF16),
        grid=(bsz, t // tm),
        in_specs=[pl.BlockSpec((1, tm, d), lambda b, i: (b, i, 0)),
                  pl.BlockSpec((1, 1, d), cmap),
                  pl.BlockSpec((1, 1, d), cmap)],
        out_specs=pl.BlockSpec((1, tm, d), lambda b, i: (b, i, 0)),
        compiler_params=_params(("arbitrary", "arbitrary")),
        name="ln_mod",
    )(x, shift, scale)


def _proj_kernel(h_ref, w_ref, o_ref):
    acc = jnp.dot(h_ref[0], w_ref[...], preferred_element_type=F32)
    for s in range(o_ref.shape[1]):
        o_ref[0, s] = acc[:, s * HEAD:(s + 1) * HEAD].astype(BF16)


def _proj(h, w, n_tiles, col_tile, tn, name):
    bsz, t, d = h.shape
    tm = _tile(t, 1024)
    sub = tn // HEAD
    return pl.pallas_call(
        _proj_kernel,
        out_shape=jax.ShapeDtypeStruct((bsz, n_tiles * sub, t, HEAD), BF16),
        grid=(bsz, t // tm, n_tiles),
        in_specs=[pl.BlockSpec((1, tm, d), lambda b, i, j: (b, i, 0)),
                  pl.BlockSpec((d, tn), lambda b, i, j: (0, col_tile(j)))],
        out_specs=pl.BlockSpec((1, sub, tm, HEAD), lambda b, i, j: (b, j, i, 0)),
        compiler_params=_params(("arbitrary", "arbitrary", "arbitrary")),
        name=name,
    )(h, w)


N_DROW = 2 * NA_KH - 1
N_DCOL = 2 * NA_KW - 1


def _na_block_kinds(rows):
    return ((0, 0), (NA_QROWS, 0), (rows - NA_QROWS, rows - NA_KROWS))


def _build_bias(rpb_ref, head, band_s, bias_s, rows):
    shape = (GRID_W, 2 * GRID_W)
    qc = lax.broadcasted_iota(jnp.int32, shape, 0)
    lane = lax.broadcasted_iota(jnp.int32, shape, 1)
    kc = lane & (GRID_W - 1)
    d_col = jnp.clip(kc - qc, -(NA_KW - 1), NA_KW - 1) + (NA_KW - 1)
    col_start = jnp.clip(qc - NA_KW // 2, 0, GRID_W - NA_KW)
    col_off = kc - col_start
    neg = jnp.full(shape, -jnp.inf, F32)
    base = head * (N_DROW * N_DCOL)
    for d in range(N_DROW):
        acc = neg
        for j in range(N_DCOL):
            acc = jnp.where(d_col == j, rpb_ref[base + d * N_DCOL + j], acc)
        band_s[d] = jnp.where(col_off < 0, neg, jnp.where(col_off < NA_KW, acc, neg))
    left = lane < GRID_W
    for kind, (r0, kb) in enumerate(_na_block_kinds(rows)):
        for qi in range(NA_QROWS):
            r = r0 + qi
            row_start = min(max(r - NA_KH // 2, 0), rows - NA_KH)

            def band(kr):
                return band_s[kr - r + NA_KH - 1] if row_start <= kr < row_start + NA_KH else neg

            for jp in range(NA_KROWS // 2):
                kr = kb + 2 * jp
                bias_s[kind, qi * GRID_W:(qi + 1) * GRID_W, jp * 2 * GRID_W:(jp + 1) * 2 * GRID_W] = (
                    jnp.where(left, band(kr), band(kr + 1)))


def _na_kernel(rpb_ref, q_ref, k_ref, v_ref, z_ref, kc_ref, vc_ref, o_ref, band_s, bias_s, *, rows):
    nq = NA_QROWS * GRID_W
    nk = NA_KROWS * GRID_W
    n_blocks = rows // NA_QROWS
    scale = HEAD ** -0.5
    kc = kc_ref[0, 0]
    vc = vc_ref[0, 0]
    nt = (((1,), (1,)), ((), ()))

    @pl.when(pl.program_id(1) == 0)
    def _():
        _build_bias(rpb_ref, pl.program_id(0), band_s, bias_s, rows)

    def block(i):
        r0 = i * NA_QROWS
        kb = jnp.clip(r0 - NA_KH // 2, 0, rows - NA_KROWS)
        kind = jnp.where(i == 0, 0, jnp.where(i == n_blocks - 1, 2, 1))
        q_off = pl.multiple_of(r0 * GRID_W, GRID_W)
        k_off = pl.multiple_of(kb * GRID_W, GRID_W)
        q = (q_ref[0, 0, pl.ds(q_off, nq), :].astype(F32) * scale).astype(BF16)
        ks = k_ref[0, 0, pl.ds(k_off, nk), :]
        vs = v_ref[0, 0, pl.ds(k_off, nk), :]
        s_loc = lax.dot_general(q, ks, nt, preferred_element_type=F32) + bias_s[kind]
        s_ctx = lax.dot_general(q, kc, nt, preferred_element_type=F32)
        m = jnp.maximum(jnp.max(s_loc, axis=-1, keepdims=True), jnp.max(s_ctx, axis=-1, keepdims=True))
        p_loc = jnp.exp(s_loc - m)
        p_ctx = jnp.exp(s_ctx - m)
        denom = jnp.sum(p_loc, axis=-1, keepdims=True) + jnp.sum(p_ctx, axis=-1, keepdims=True)
        o = (jnp.dot(p_loc.astype(BF16), vs, preferred_element_type=F32)
             + jnp.dot(p_ctx.astype(BF16), vc, preferred_element_type=F32))
        o = o / denom
        z = z_ref[0, 0, pl.ds(q_off, nq), :].astype(F32)
        o_ref[0, 0, pl.ds(q_off, nq), :] = (o * (z * jax.nn.sigmoid(z))).astype(BF16)

    def blocks(n, carry):
        for u in range(NA_UNROLL):
            block(n * NA_UNROLL + u)
        return carry

    lax.fori_loop(0, n_blocks // NA_UNROLL, blocks, 0)


def _na(proj, proj_ctx, rpb, nh):
    bsz, _, t, _ = proj.shape
    lc = proj_ctx.shape[2]
    rows = t // GRID_W
    assert rows % (NA_QROWS * NA_UNROLL) == 0 and rows >= NA_KROWS + NA_QROWS
    assert rpb.shape == (nh, N_DROW, N_DCOL)
    lat = lambda g: pl.BlockSpec((1, 1, t, HEAD), lambda h, b: (b, g * nh + h, 0, 0))
    ctx = lambda g: pl.BlockSpec((1, 1, lc, HEAD), lambda h, b: (b, g * nh + h, 0, 0))
    return pl.pallas_call(
        functools.partial(_na_kernel, rows=rows),
        out_shape=jax.ShapeDtypeStruct((bsz, nh, t, HEAD), BF16),
        grid=(nh, bsz),
        in_specs=[pl.BlockSpec(memory_space=pltpu.SMEM),
                  lat(0), lat(1), lat(2), lat(3), ctx(0), ctx(1)],
        out_specs=pl.BlockSpec((1, 1, t, HEAD), lambda h, b: (b, h, 0, 0)),
        scratch_shapes=[pltpu.VMEM((N_DROW, GRID_W, 2 * GRID_W), F32),
                        pltpu.VMEM((3, NA_QROWS * GRID_W, NA_KROWS * GRID_W), F32)],
        compiler_params=_params(("arbitrary", "arbitrary")),
        name="na",
    )(rpb.astype(F32).reshape(-1), proj, proj, proj, proj, proj_ctx, proj_ctx)


def _chunk_cumsum(x, reverse):
    row = lax.broadcasted_iota(jnp.int32, x.shape, 0)
    s = 1
    while s < CHUNK:
        if reverse:
            x = x + jnp.where(row < CHUNK - s, pltpu.roll(x, CHUNK - s, 0), 0.0)
        else:
            x = x + jnp.where(row >= s, pltpu.roll(x, s, 0), 0.0)
        s *= 2
    return x


def _forget(f_pre, lb):
    sg = jax.nn.sigmoid(f_pre)
    one_m_lb = 1.0 - lb
    return one_m_lb * (1.0 - sg), jnp.log(lb + one_m_lb * sg)


def _lower_bound(logits):
    m = jnp.max(logits, axis=0, keepdims=True)
    e = jnp.exp(logits - m)
    return e[0:1] / jnp.sum(e, axis=0, keepdims=True)


def _state_chunk(st, f_pre, v, lb, reverse):
    kk, logf = _forget(f_pre, lb)
    cum = _chunk_cumsum(logf, reverse)
    last = cum[0:1] if reverse else cum[CHUNK - 1:CHUNK]
    kl = (kk * jnp.exp(last - cum)).astype(BF16)
    tn = (((0,), (0,)), ((), ()))
    u_t = lax.dot_general(v, kl, tn, preferred_element_type=F32)
    return st * jnp.exp(last) + u_t


def _full_chunk(st, q_pre, f_pre, v, lb, reverse, tri):
    kk, logf = _forget(f_pre, lb)
    cum = _chunk_cumsum(logf, reverse)
    mid = CHUNK // 2
    ref = cum[mid:mid + 1] if reverse else cum[mid - 1:mid]
    last = cum[0:1] if reverse else cum[CHUNK - 1:CHUNK]
    qs = q_pre * jax.nn.sigmoid(q_pre)
    qd = qs * jnp.exp(cum - ref)
    kd = kk * jnp.exp(ref - cum)
    nt = (((1,), (1,)), ((), ()))
    tn = (((0,), (0,)), ((), ()))
    a = lax.dot_general(qd.astype(BF16), kd.astype(BF16), nt, preferred_element_type=F32)
    a = jnp.where(tri, a, 0.0)
    o = jnp.dot(a.astype(BF16), v, preferred_element_type=F32)
    qe = (qd * jnp.exp(ref)).astype(BF16)
    o = o + lax.dot_general(qe, st.astype(BF16), nt, preferred_element_type=F32)
    kl = (kd * jnp.exp(last - ref)).astype(BF16)
    u_t = lax.dot_general(v, kl, tn, preferred_element_type=F32)
    return o, st * jnp.exp(last) + u_t


def _hg_kernel(q_ref, ff_ref, fb_ref, i_ref, g_ref, ffc_ref, fbc_ref, ic_ref,
               lbf_ref, lbb_ref, nw_ref, o_ref, ob_ref, *, t, lc, unroll):
    n_lat = t // CHUNK
    n_ctx = lc // CHUNK
    lb_f = _lower_bound(lbf_ref[...])
    lb_b = _lower_bound(lbb_ref[...])
    nw = nw_ref[...]
    r_i = lax.broadcasted_iota(jnp.int32, (CHUNK, CHUNK), 0)
    c_i = lax.broadcasted_iota(jnp.int32, (CHUNK, CHUNK), 1)
    tri_f = r_i >= c_i
    tri_b = r_i <= c_i

    def sl(ref, c):
        return ref[0, 0, pl.ds(pl.multiple_of(c * CHUNK, CHUNK), CHUNK), :]

    st0 = jnp.zeros((HEAD, HEAD), F32)

    def ctx_b(n, st):
        c = n_ctx - 1 - n
        return _state_chunk(st, sl(fbc_ref, c).astype(F32), sl(ic_ref, c), lb_b, True)

    st = lax.fori_loop(0, n_ctx, ctx_b, st0)

    def lat_b(n, st):
        for u in range(unroll):
            c = n_lat - 1 - (n * unroll + u)
            o, st = _full_chunk(st, sl(q_ref, c).astype(F32), sl(fb_ref, c).astype(F32), sl(i_ref, c),
                                lb_b, True, tri_b)
            ob_ref[pl.ds(pl.multiple_of(c * CHUNK, CHUNK), CHUNK), :] = o
        return st

    lax.fori_loop(0, n_lat // unroll, lat_b, st)

    def ctx_f(c, st):
        return _state_chunk(st, sl(ffc_ref, c).astype(F32), sl(ic_ref, c), lb_f, False)

    st = lax.fori_loop(0, n_ctx, ctx_f, st0)

    def lat_f(n, st):
        for u in range(unroll):
            c = n * unroll + u
            off = pl.multiple_of(c * CHUNK, CHUNK)
            o, st = _full_chunk(st, sl(q_ref, c).astype(F32), sl(ff_ref, c).astype(F32), sl(i_ref, c),
                                lb_f, False, tri_f)
            o = o + ob_ref[pl.ds(off, CHUNK), :]
            o = o * lax.rsqrt(jnp.mean(o * o, axis=-1, keepdims=True) + RMS_EPS) * nw
            g = sl(g_ref, c).astype(F32)
            o_ref[0, 0, pl.ds(off, CHUNK), :] = (o * (g * jax.nn.sigmoid(g))).astype(BF16)
        return st

    lax.fori_loop(0, n_lat // unroll, lat_f, st)


def _hgrn2(proj, proj_ctx, lb_f, lb_b, norm_w, nh):
    bsz, _, t, _ = proj.shape
    lc = proj_ctx.shape[2]
    n_slots = lb_f.shape[0]
    unroll = 8
    assert t % (CHUNK * unroll) == 0 and lc % CHUNK == 0
    lat = lambda g: pl.BlockSpec((1, 1, t, HEAD), lambda b, h: (b, g * nh + h, 0, 0))
    ctx = lambda g: pl.BlockSpec((1, 1, lc, HEAD), lambda b, h: (b, g * nh + h, 0, 0))
    lbs = pl.BlockSpec((n_slots, HEAD), lambda b, h: (0, h))
    return pl.pallas_call(
        functools.partial(_hg_kernel, t=t, lc=lc, unroll=unroll),
        out_shape=jax.ShapeDtypeStruct((bsz, nh, t, HEAD), BF16),
        grid=(bsz, nh),
        in_specs=[lat(4), lat(5), lat(6), lat(7), lat(8), ctx(2), ctx(3), ctx(4), lbs, lbs,
                  pl.BlockSpec((1, HEAD), lambda b, h: (0, 0))],
        out_specs=pl.BlockSpec((1, 1, t, HEAD), lambda b, h: (b, h, 0, 0)),
        scratch_shapes=[pltpu.VMEM((t, HEAD), F32)],
        compiler_params=_params(("arbitrary", "arbitrary")),
        name="hgrn2",
    )(proj, proj, proj, proj, proj, proj_ctx, proj_ctx, proj_ctx, lb_f, lb_b, norm_w)


def _merge_kernel(ya_ref, yb_ref, wa_ref, wb_ref, ga_ref, gb_ref, o_ref, ya_s, yb_s):
    @pl.when(pl.program_id(2) == 0)
    def _():
        for h in range(ya_ref.shape[1]):
            ya_s[:, h * HEAD:(h + 1) * HEAD] = ya_ref[0, h]
            yb_s[:, h * HEAD:(h + 1) * HEAD] = yb_ref[0, h]

    ta = jnp.dot(ya_s[...], wa_ref[...], preferred_element_type=F32)
    tb = jnp.dot(yb_s[...], wb_ref[...], preferred_element_type=F32)
    for s in range(ga_ref.shape[1]):
        cs = slice(s * HEAD, (s + 1) * HEAD)
        m = (jax.nn.sigmoid(ga_ref[0, s].astype(F32)) * ta[:, cs]
             + jax.nn.sigmoid(gb_ref[0, s].astype(F32)) * tb[:, cs])
        o_ref[0, :, cs] = m.astype(BF16)


def _merge(ya, yb, w_pa, w_pb, proj, nh):
    bsz, _, t, _ = ya.shape
    d = w_pa.shape[1]
    tm = _tile(t, 512)
    tn = _tile(nh * HEAD, 1024)
    sub = tn // HEAD
    ga0 = 9 * nh // sub
    gb0 = 11 * nh // sub
    assert (9 * nh) % sub == 0 and (11 * nh) % sub == 0
    yspec = pl.BlockSpec((1, nh, tm, HEAD), lambda b, i, j: (b, 0, i, 0))
    wspec = pl.BlockSpec((nh * HEAD, tn), lambda b, i, j: (0, j))
    return pl.pallas_call(
        _merge_kernel,
        out_shape=jax.ShapeDtypeStruct((bsz, t, d), BF16),
        grid=(bsz, t // tm, d // tn),
        in_specs=[yspec, yspec, wspec, wspec,
                  pl.BlockSpec((1, sub, tm, HEAD), lambda b, i, j: (b, ga0 + j, i, 0)),
                  pl.BlockSpec((1, sub, tm, HEAD), lambda b, i, j: (b, gb0 + j, i, 0))],
        out_specs=pl.BlockSpec((1, tm, tn), lambda b, i, j: (b, i, j)),
        scratch_shapes=[pltpu.VMEM((tm, nh * HEAD), BF16), pltpu.VMEM((tm, nh * HEAD), BF16)],
        compiler_params=_params(("arbitrary", "arbitrary", "arbitrary")),
        name="merge",
    )(ya, yb, w_pa, w_pb, proj, proj)


def _out_kernel(m_ref, w_ref, x_ref, gate_ref, lng_ref, lnb_ref, o_ref, *, alpha, tn):
    j = pl.program_id(2)
    out = jnp.dot(m_ref[0], w_ref[...], preferred_element_type=F32)
    col = pl.multiple_of(j * tn, tn)
    o_ref[0, :, pl.ds(col, tn)] = alpha * x_ref[0] + gate_ref[0] * out

    @pl.when(j == pl.num_programs(2) - 1)
    def _():
        def norm_rows(i, carry):
            rs = pl.ds(pl.multiple_of(i * LN_ROWS, LN_ROWS), LN_ROWS)
            r = o_ref[0, rs, :]
            mu = jnp.mean(r, axis=-1, keepdims=True)
            rc = r - mu
            var = jnp.mean(rc * rc, axis=-1, keepdims=True)
            o_ref[0, rs, :] = rc * lax.rsqrt(var + LN_EPS) * lng_ref[...] + lnb_ref[...]
            return carry

        lax.fori_loop(0, o_ref.shape[1] // LN_ROWS, norm_rows, 0)


def _out(m, w_out, x, gate, ln_g, ln_b, alpha):
    bsz, t, d = x.shape
    tm = _tile(t, 512)
    tn = _tile(d, 1024)
    return pl.pallas_call(
        functools.partial(_out_kernel, alpha=alpha, tn=tn),
        out_shape=jax.ShapeDtypeStruct((bsz, t, d), F32),
        grid=(bsz, t // tm, d // tn),
        in_specs=[pl.BlockSpec((1, tm, d), lambda b, i, j: (b, i, 0)),
                  pl.BlockSpec((d, tn), lambda b, i, j: (0, j)),
                  pl.BlockSpec((1, tm, tn), lambda b, i, j: (b, i, j)),
                  pl.BlockSpec((1, 1, tn), lambda b, i, j: (b, 0, j)),
                  pl.BlockSpec((1, d), lambda b, i, j: (0, 0)),
                  pl.BlockSpec((1, d), lambda b, i, j: (0, 0))],
        out_specs=pl.BlockSpec((1, tm, d), lambda b, i, j: (b, i, 0)),
        compiler_params=_params(("arbitrary", "arbitrary", "arbitrary")),
        name="out",
    )(m, w_out, x, gate, ln_g, ln_b)


def _layer(x, ctx, cond, w_ada, b_ada, w_in, rpb, lb_f, lb_b, norm_w, w_pa, w_pb, w_out, ln_g, ln_b, alpha):
    bsz, t, d = x.shape
    nh = (d // 2) // HEAD
    grp = d // 2
    tn = _tile(grp, 1024)
    per = grp // tn

    mod = _adaln(cond, w_ada, b_ada.reshape(1, -1))
    shift, scale, gate = (mod[:3, k * d:(k + 1) * d].reshape(3, 1, d) for k in range(3))
    ctx_row = bsz

    h_lat = _ln_mod(x, shift, scale, lambda b: b)
    h_ctx = _ln_mod(ctx, shift, scale, lambda b: ctx_row)

    w_in16 = w_in.astype(BF16)
    proj = _proj(h_lat, w_in16, 13 * per, lambda j: j, tn, "proj_lat")
    proj_ctx = _proj(h_ctx, w_in16, 5 * per, lambda j: jnp.where(j < 2 * per, j + per, j + 3 * per), tn,
                     "proj_ctx")

    rows = t // GRID_W
    y_a = _na(proj, proj_ctx, rpb, nh)
    y_b = _hgrn2(proj, proj_ctx, lb_f, lb_b, norm_w.reshape(1, HEAD), nh)
    m = _merge(y_a, y_b, w_pa.astype(BF16), w_pb.astype(BF16), proj, nh)
    return _out(m, w_out.astype(BF16), x, gate, ln_g.reshape(1, d), ln_b.reshape(1, d), alpha)


def kernel(x, c, ctx, c_ctx, w_ada, b_ada, w_in, na_rpb, hg_lb_fwd, hg_lb_bwd, hg_norm_w, w_pa, w_pb, w_out,
           ln_g, ln_b):
    depth = w_ada.shape[0]
    assert depth == 1, "the context stream update of deeper stacks is not implemented"
    bsz, _, d = x.shape
    alpha = (2.0 * depth) ** 0.25
    cond = jnp.concatenate([c, c_ctx[None], jnp.zeros((8 - bsz - 1, d), c.dtype)], axis=0)
    return _layer(x, ctx, cond, w_ada[0], b_ada[0], w_in[0], na_rpb[0], hg_lb_fwd, hg_lb_bwd, hg_norm_w[0],
                  w_pa[0], w_pb[0], w_out[0], ln_g[0], ln_b[0], alpha)
```

```python
import functools
import math

import jax
import jax.numpy as jnp
from jax import lax
from jax.experimental import pallas as pl
from jax.experimental.pallas import tpu as pltpu

F32 = jnp.float32
BF16 = jnp.bfloat16

GRID_W = 64
HEAD = 128
NA_KH = 8
NA_KW = 16
NA_QROWS = 4
NA_KROWS = 12
NA_UNROLL = 2
CHUNK = 64
HG_CHUNKS = 8
HG_TILE = HG_CHUNKS * CHUNK
LOG2E = math.log2(math.e)
LN_EPS = 1e-6
RMS_EPS = 1e-6
LN_ROWS = 64
VMEM_LIMIT = 56 * 1024 * 1024


def _params(semantics, vmem=VMEM_LIMIT):
    return pltpu.CompilerParams(dimension_semantics=semantics, vmem_limit_bytes=vmem)


def _tile(n, pref):
    t = min(n, pref)
    assert n % t == 0, (n, t)
    return t


def _adaln_kernel(c_ref, w_ref, b_ref, o_ref):
    c = c_ref[...]
    s = c * jax.nn.sigmoid(c)
    s_hi = s.astype(BF16)
    s_lo = (s - s_hi.astype(F32)).astype(BF16)
    w = w_ref[...]
    w_hi = w.astype(BF16)
    w_lo = (w - w_hi.astype(F32)).astype(BF16)
    n = s.shape[0]
    lhs = jnp.concatenate([s_hi, s_lo], axis=0)
    a = jnp.dot(lhs, w_hi, preferred_element_type=F32)
    b = jnp.dot(s_hi, w_lo, preferred_element_type=F32)
    o_ref[...] = a[:n] + a[n:] + b + b_ref[...]


def _adaln(cond, w, b):
    d, n = w.shape
    tn = _tile(n, 512)
    return pl.pallas_call(
        _adaln_kernel,
        out_shape=jax.ShapeDtypeStruct((cond.shape[0], n), F32),
        grid=(n // tn,),
        in_specs=[pl.BlockSpec(cond.shape, lambda j: (0, 0)),
                  pl.BlockSpec((d, tn), lambda j: (0, j)),
                  pl.BlockSpec((1, tn), lambda j: (0, j))],
        out_specs=pl.BlockSpec((cond.shape[0], tn), lambda j: (0, j)),
        compiler_params=_params(("arbitrary",)),
        name="adaln",
    )(cond, w, b)


def _ln_mod_kernel(x_ref, shift_ref, scale_ref, o_ref):
    x = x_ref[0]
    mu = jnp.mean(x, axis=-1, keepdims=True)
    xc = x - mu
    var = jnp.mean(xc * xc, axis=-1, keepdims=True)
    y = xc * lax.rsqrt(var + LN_EPS)
    o_ref[0] = (y * (1.0 + scale_ref[0]) + shift_ref[0]).astype(BF16)


def _ln_mod(x, shift, scale, cond_row):
    bsz, t, d = x.shape
    tm = _tile(t, 256)
    cmap = lambda b, i: (cond_row(b), 0, 0)
    return pl.pallas_call(
        _ln_mod_kernel,
        out_shape=jax.ShapeDtypeStruct((bsz, t, d), BF16),
        grid=(bsz, t // tm),
        in_specs=[pl.BlockSpec((1, tm, d), lambda b, i: (b, i, 0)),
                  pl.BlockSpec((1, 1, d), cmap),
                  pl.BlockSpec((1, 1, d), cmap)],
        out_specs=pl.BlockSpec((1, tm, d), lambda b, i: (b, i, 0)),
        compiler_params=_params(("arbitrary", "arbitrary")),
        name="ln_mod",
    )(x, shift, scale)


def _proj_kernel(h_ref, w_ref, o_ref):
    acc = jnp.dot(h_ref[0], w_ref[...], preferred_element_type=F32)
    for s in range(o_ref.shape[1]):
        o_ref[0, s] = acc[:, s * HEAD:(s + 1) * HEAD].astype(BF16)


def _proj(h, w, n_tiles, col_tile, tn, name):
    bsz, t, d = h.shape
    tm = _tile(t, 1024)
    sub = tn // HEAD
    return pl.pallas_call(
        _proj_kernel,
        out_shape=jax.ShapeDtypeStruct((bsz, n_tiles * sub, t, HEAD), BF16),
        grid=(bsz, t // tm, n_tiles),
        in_specs=[pl.BlockSpec((1, tm, d), lambda b, i, j: (b, i, 0)),
                  pl.BlockSpec((d, tn), lambda b, i, j: (0, col_tile(j)))],
        out_specs=pl.BlockSpec((1, sub, tm, HEAD), lambda b, i, j: (b, j, i, 0)),
        compiler_params=_params(("arbitrary", "arbitrary", "arbitrary")),
        name=name,
    )(h, w)


N_DROW = 2 * NA_KH - 1
N_DCOL = 2 * NA_KW - 1


def _na_block_kinds(rows):
    return ((0, 0), (NA_QROWS, 0), (rows - NA_QROWS, rows - NA_KROWS))


def _build_bias(rpb_ref, head, band_s, bias_s, rows):
    shape = (GRID_W, 2 * GRID_W)
    qc = lax.broadcasted_iota(jnp.int32, shape, 0)
    lane = lax.broadcasted_iota(jnp.int32, shape, 1)
    kc = lane & (GRID_W - 1)
    d_col = jnp.clip(kc - qc, -(NA_KW - 1), NA_KW - 1) + (NA_KW - 1)
    col_start = jnp.clip(qc - NA_KW // 2, 0, GRID_W - NA_KW)
    col_off = kc - col_start
    neg = jnp.full(shape, -jnp.inf, F32)
    base = head * (N_DROW * N_DCOL)
    for d in range(N_DROW):
        acc = neg
        for j in range(N_DCOL):
            acc = jnp.where(d_col == j, rpb_ref[base + d * N_DCOL + j] * LOG2E, acc)
        band_s[d] = jnp.where(col_off < 0, neg, jnp.where(col_off < NA_KW, acc, neg))
    left = lane < GRID_W
    for kind, (r0, kb) in enumerate(_na_block_kinds(rows)):
        for qi in range(NA_QROWS):
            r = r0 + qi
            row_start = min(max(r - NA_KH // 2, 0), rows - NA_KH)

            def band(kr):
                return band_s[kr - r + NA_KH - 1] if row_start <= kr < row_start + NA_KH else neg

            for jp in range(NA_KROWS // 2):
                kr = kb + 2 * jp
                bias_s[kind, qi * GRID_W:(qi + 1) * GRID_W, jp * 2 * GRID_W:(jp + 1) * 2 * GRID_W] = (
                    jnp.where(left, band(kr), band(kr + 1)))


def _na_kernel(rpb_ref, q_ref, k_ref, v_ref, z_ref, kc_ref, vc_ref, o_ref, band_s, bias_s, *, rows):
    nq = NA_QROWS * GRID_W
    nk = NA_KROWS * GRID_W
    n_blocks = rows // NA_QROWS
    scale = HEAD ** -0.5 * LOG2E
    kc = kc_ref[0, 0]
    vc = vc_ref[0, 0]
    nt = (((1,), (1,)), ((), ()))

    @pl.when(pl.program_id(1) == 0)
    def _():
        _build_bias(rpb_ref, pl.program_id(0), band_s, bias_s, rows)

    def scores(i):
        r0 = i * NA_QROWS
        kb = jnp.clip(r0 - NA_KH // 2, 0, rows - NA_KROWS)
        kind = jnp.where(i == 0, 0, jnp.where(i == n_blocks - 1, 2, 1))
        q_off = pl.multiple_of(r0 * GRID_W, GRID_W)
        k_off = pl.multiple_of(kb * GRID_W, GRID_W)
        q = (q_ref[0, 0, pl.ds(q_off, nq), :].astype(F32) * scale).astype(BF16)
        s_loc = lax.dot_general(q, k_ref[0, 0, pl.ds(k_off, nk), :], nt, preferred_element_type=F32)
        s_ctx = lax.dot_general(q, kc, nt, preferred_element_type=F32)
        return q_off, k_off, s_loc + bias_s[kind], s_ctx

    def attend(q_off, k_off, s_loc, s_ctx):
        m = jnp.maximum(jnp.max(s_loc, axis=-1, keepdims=True), jnp.max(s_ctx, axis=-1, keepdims=True))
        p_loc = jnp.exp2(s_loc - m)
        p_ctx = jnp.exp2(s_ctx - m)
        denom = jnp.sum(p_loc, axis=-1, keepdims=True) + jnp.sum(p_ctx, axis=-1, keepdims=True)
        o = (jnp.dot(p_loc.astype(BF16), v_ref[0, 0, pl.ds(k_off, nk), :], preferred_element_type=F32)
             + jnp.dot(p_ctx.astype(BF16), vc, preferred_element_type=F32))
        o = o / denom
        z = z_ref[0, 0, pl.ds(q_off, nq), :].astype(F32)
        o_ref[0, 0, pl.ds(q_off, nq), :] = (o * (z * jax.nn.sigmoid(z))).astype(BF16)

    def blocks(n, carry):
        staged = [scores(n * NA_UNROLL + u) for u in range(NA_UNROLL)]
        for args in staged:
            attend(*args)
        return carry

    lax.fori_loop(0, n_blocks // NA_UNROLL, blocks, 0)


def _na(proj, proj_ctx, rpb, nh):
    bsz, _, t, _ = proj.shape
    lc = proj_ctx.shape[2]
    rows = t // GRID_W
    assert rows % (NA_QROWS * NA_UNROLL) == 0 and rows >= NA_KROWS + NA_QROWS
    assert rpb.shape == (nh, N_DROW, N_DCOL)
    lat = lambda g: pl.BlockSpec((1, 1, t, HEAD), lambda h, b: (b, g * nh + h, 0, 0))
    ctx = lambda g: pl.BlockSpec((1, 1, lc, HEAD), lambda h, b: (b, g * nh + h, 0, 0))
    return pl.pallas_call(
        functools.partial(_na_kernel, rows=rows),
        out_shape=jax.ShapeDtypeStruct((bsz, nh, t, HEAD), BF16),
        grid=(nh, bsz),
        in_specs=[pl.BlockSpec(memory_space=pltpu.SMEM),
                  lat(0), lat(1), lat(2), lat(3), ctx(0), ctx(1)],
        out_specs=pl.BlockSpec((1, 1, t, HEAD), lambda h, b: (b, h, 0, 0)),
        scratch_shapes=[pltpu.VMEM((N_DROW, GRID_W, 2 * GRID_W), F32),
                        pltpu.VMEM((3, NA_QROWS * GRID_W, NA_KROWS * GRID_W), F32)],
        compiler_params=_params(("arbitrary", "arbitrary")),
        name="na",
    )(rpb.astype(F32).reshape(-1), proj, proj, proj, proj, proj_ctx, proj_ctx)


def _seg_cumsum(x, reverse):
    r_i = lax.broadcasted_iota(jnp.int32, (CHUNK, 2 * CHUNK), 0)
    c_i = lax.broadcasted_iota(jnp.int32, (CHUNK, 2 * CHUNK), 1) & (CHUNK - 1)
    tri2 = jnp.where((r_i <= c_i) if reverse else (r_i >= c_i), 1.0, 0.0).astype(BF16)
    hi = x.astype(BF16)
    lo = (x - hi.astype(F32)).astype(BF16)
    out = []
    for c in range(x.shape[0] // CHUNK):
        rows = slice(c * CHUNK, (c + 1) * CHUNK)
        out.append(jnp.dot(tri2, jnp.concatenate([hi[rows], lo[rows]], axis=0), preferred_element_type=F32))
    return jnp.concatenate(out, axis=0)


def _forget(f_pre, lb):
    sg = jax.nn.sigmoid(f_pre)
    one_m_lb = 1.0 - lb
    return one_m_lb * (1.0 - sg), jnp.log(lb + one_m_lb * sg)


def _lower_bound(logits):
    m = jnp.max(logits, axis=0, keepdims=True)
    e = jnp.exp(logits - m)
    return e[0:1] / jnp.sum(e, axis=0, keepdims=True)


def _gates(f_pre, lb, reverse):
    kk, logf = _forget(f_pre, lb)
    cum = _seg_cumsum(logf, reverse).reshape(-1, CHUNK, HEAD)
    mid = CHUNK // 2
    ref = cum[:, mid:mid + 1] if reverse else cum[:, mid - 1:mid]
    last = cum[:, 0:1] if reverse else cum[:, CHUNK - 1:CHUNK]
    return kk.reshape(cum.shape), cum, ref, last


def _ctx_state(f_pre, v, lb, reverse):
    kk, cum, _, last = _gates(f_pre, lb, reverse)
    kl = (kk * jnp.exp(last - cum)).astype(BF16)
    dec = jnp.exp(last)
    tn = (((0,), (0,)), ((), ()))
    n_chunks = kl.shape[0]
    st = jnp.zeros((HEAD, HEAD), F32)
    for c in (reversed(range(n_chunks)) if reverse else range(n_chunks)):
        u_t = lax.dot_general(v[c * CHUNK:(c + 1) * CHUNK], kl[c], tn, preferred_element_type=F32)
        st = st * dec[c] + u_t
    return st


def _hg_prepare(qs, f_pre, lb, reverse):
    kk, cum, ref, last = _gates(f_pre, lb, reverse)
    e = jnp.exp(cum - ref)
    qd = qs.reshape(cum.shape) * e
    kd = kk / e
    qe = qd * jnp.exp(ref)
    kl = kd * jnp.exp(last - ref)
    flat = lambda a: a.reshape(-1, HEAD).astype(BF16)
    return flat(qd), flat(kd), flat(qe), flat(kl), jnp.exp(last).reshape(-1, HEAD)


def _hg_kernel(q_ref, ff_ref, fb_ref, i_ref, g_ref, ffc_ref, fbc_ref, ic_ref, lbf_ref, lbb_ref, nw_ref,
               o_ref, qs_s, qd_s, kd_s, qe_s, kl_s, dec_s, ob_s, *, t):
    n_tiles = t // HG_TILE
    lb_f = _lower_bound(lbf_ref[...])
    lb_b = _lower_bound(lbb_ref[...])
    nw = nw_ref[...]
    r_i = lax.broadcasted_iota(jnp.int32, (CHUNK, CHUNK), 0)
    c_i = lax.broadcasted_iota(jnp.int32, (CHUNK, CHUNK), 1)
    nt = (((1,), (1,)), ((), ()))
    tn = (((0,), (0,)), ((), ()))

    def prepare(tix, f_ref, lb, reverse, first):
        rs = pl.ds(pl.multiple_of(tix * HG_TILE, HG_TILE), HG_TILE)
        if first:
            q = q_ref[0, 0, rs, :].astype(F32)
            qs = q * jax.nn.sigmoid(q)
        else:
            qs = qs_s[rs, :]
        return (qs if first else None,) + _hg_prepare(qs, f_ref[0, 0, rs, :].astype(F32), lb, reverse)

    def stage(tix, vals):
        rs = pl.ds(pl.multiple_of(tix * HG_TILE, HG_TILE), HG_TILE)
        qs, qd, kd, qe, kl, dec = vals
        if qs is not None:
            qs_s[rs, :] = qs
        qd_s[rs, :] = qd
        kd_s[rs, :] = kd
        qe_s[rs, :] = qe
        kl_s[rs, :] = kl
        dec_s[tix] = dec

    def scan(tix, st, reverse, emit):
        tri = (r_i <= c_i) if reverse else (r_i >= c_i)
        order = [HG_CHUNKS - 1 - u if reverse else u for u in range(HG_CHUNKS)]
        cs = {j: pl.ds(pl.multiple_of(tix * HG_TILE + j * CHUNK, CHUNK), CHUNK) for j in order}
        v = {j: i_ref[0, 0, cs[j], :] for j in order}
        a = {j: lax.dot_general(qd_s[cs[j], :], kd_s[cs[j], :], nt, preferred_element_type=F32) for j in order}
        u_t = {j: lax.dot_general(v[j], kl_s[cs[j], :], tn, preferred_element_type=F32) for j in order}
        dec = dec_s[tix]
        for j in order:
            o = (jnp.dot(jnp.where(tri, a[j], 0.0).astype(BF16), v[j], preferred_element_type=F32)
                 + lax.dot_general(qe_s[cs[j], :], st.astype(BF16), nt, preferred_element_type=F32))
            emit(cs[j], o)
            st = st * dec[j:j + 1] + u_t[j]
        return st

    def direction(f_ref, fc_ref, lb, reverse, first, emit):
        tile_of = (lambda n: n_tiles - 1 - n) if reverse else (lambda n: n)
        stage(tile_of(0), prepare(tile_of(0), f_ref, lb, reverse, first))
        st = _ctx_state(fc_ref[0, 0].astype(F32), ic_ref[0, 0], lb, reverse)

        def trip(n, st):
            nxt = tile_of(jnp.minimum(n + 1, n_tiles - 1))
            vals = prepare(nxt, f_ref, lb, reverse, first)
            st = scan(tile_of(n), st, reverse, emit)
            stage(nxt, vals)
            return st

        lax.fori_loop(0, n_tiles, trip, st)

    def emit_backward(cs, o):
        ob_s[cs, :] = o

    def emit_forward(cs, o):
        o = o + ob_s[cs, :]
        o = o * lax.rsqrt(jnp.mean(o * o, axis=-1, keepdims=True) + RMS_EPS) * nw
        g = g_ref[0, 0, cs, :].astype(F32)
        o_ref[0, 0, cs, :] = (o * (g * jax.nn.sigmoid(g))).astype(BF16)

    direction(fb_ref, fbc_ref, lb_b, True, True, emit_backward)
    direction(ff_ref, ffc_ref, lb_f, False, False, emit_forward)


def _hgrn2(proj, proj_ctx, lb_f, lb_b, norm_w, nh):
    bsz, _, t, _ = proj.shape
    lc = proj_ctx.shape[2]
    n_slots = lb_f.shape[0]
    assert t % HG_TILE == 0 and lc % CHUNK == 0
    lat = lambda g: pl.BlockSpec((1, 1, t, HEAD), lambda b, h: (b, g * nh + h, 0, 0))
    ctx = lambda g: pl.BlockSpec((1, 1, lc, HEAD), lambda b, h: (b, g * nh + h, 0, 0))
    lbs = pl.BlockSpec((n_slots, HEAD), lambda b, h: (0, h))
    staged = pltpu.VMEM((t, HEAD), BF16)
    return pl.pallas_call(
        functools.partial(_hg_kernel, t=t),
        out_shape=jax.ShapeDtypeStruct((bsz, nh, t, HEAD), BF16),
        grid=(bsz, nh),
        in_specs=[lat(4), lat(5), lat(6), lat(7), lat(8), ctx(2), ctx(3), ctx(4), lbs, lbs,
                  pl.BlockSpec((1, HEAD), lambda b, h: (0, 0))],
        out_specs=pl.BlockSpec((1, 1, t, HEAD), lambda b, h: (b, h, 0, 0)),
        scratch_shapes=[pltpu.VMEM((t, HEAD), F32), staged, staged, staged, staged,
                        pltpu.VMEM((t // HG_TILE, HG_CHUNKS, HEAD), F32), pltpu.VMEM((t, HEAD), F32)],
        compiler_params=_params(("arbitrary", "arbitrary")),
        name="hgrn2",
    )(proj, proj, proj, proj, proj, proj_ctx, proj_ctx, proj_ctx, lb_f, lb_b, norm_w)


def _merge_kernel(ya_ref, yb_ref, wa_ref, wb_ref, ga_ref, gb_ref, o_ref, ya_s, yb_s):
    @pl.when(pl.program_id(2) == 0)
    def _():
        for h in range(ya_ref.shape[1]):
            ya_s[:, h * HEAD:(h + 1) * HEAD] = ya_ref[0, h]
            yb_s[:, h * HEAD:(h + 1) * HEAD] = yb_ref[0, h]

    ta = jnp.dot(ya_s[...], wa_ref[...], preferred_element_type=F32)
    tb = jnp.dot(yb_s[...], wb_ref[...], preferred_element_type=F32)
    for s in range(ga_ref.shape[1]):
        cs = slice(s * HEAD, (s + 1) * HEAD)
        m = (jax.nn.sigmoid(ga_ref[0, s].astype(F32)) * ta[:, cs]
             + jax.nn.sigmoid(gb_ref[0, s].astype(F32)) * tb[:, cs])
        o_ref[0, :, cs] = m.astype(BF16)


def _merge(ya, yb, w_pa, w_pb, proj, nh):
    bsz, _, t, _ = ya.shape
    d = w_pa.shape[1]
    tm = _tile(t, 512)
    tn = _tile(nh * HEAD, 1024)
    sub = tn // HEAD
    ga0 = 9 * nh // sub
    gb0 = 11 * nh // sub
    assert (9 * nh) % sub == 0 and (11 * nh) % sub == 0
    yspec = pl.BlockSpec((1, nh, tm, HEAD), lambda b, i, j: (b, 0, i, 0))
    wspec = pl.BlockSpec((nh * HEAD, tn), lambda b, i, j: (0, j))
    return pl.pallas_call(
        _merge_kernel,
        out_shape=jax.ShapeDtypeStruct((bsz, t, d), BF16),
        grid=(bsz, t // tm, d // tn),
        in_specs=[yspec, yspec, wspec, wspec,
                  pl.BlockSpec((1, sub, tm, HEAD), lambda b, i, j: (b, ga0 + j, i, 0)),
                  pl.BlockSpec((1, sub, tm, HEAD), lambda b, i, j: (b, gb0 + j, i, 0))],
        out_specs=pl.BlockSpec((1, tm, tn), lambda b, i, j: (b, i, j)),
        scratch_shapes=[pltpu.VMEM((tm, nh * HEAD), BF16), pltpu.VMEM((tm, nh * HEAD), BF16)],
        compiler_params=_params(("arbitrary", "arbitrary", "arbitrary")),
        name="merge",
    )(ya, yb, w_pa, w_pb, proj, proj)


def _out_kernel(m_ref, w_ref, x_ref, gate_ref, lng_ref, lnb_ref, o_ref, *, alpha, tn):
    j = pl.program_id(2)
    out = jnp.dot(m_ref[0], w_ref[...], preferred_element_type=F32)
    col = pl.multiple_of(j * tn, tn)
    o_ref[0, :, pl.ds(col, tn)] = alpha * x_ref[0] + gate_ref[0] * out

    @pl.when(j == pl.num_programs(2) - 1)
    def _():
        def norm_rows(i, carry):
            rs = pl.ds(pl.multiple_of(i * LN_ROWS, LN_ROWS), LN_ROWS)
            r = o_ref[0, rs, :]
            mu = jnp.mean(r, axis=-1, keepdims=True)
            rc = r - mu
            var = jnp.mean(rc * rc, axis=-1, keepdims=True)
            o_ref[0, rs, :] = rc * lax.rsqrt(var + LN_EPS) * lng_ref[...] + lnb_ref[...]
            return carry

        lax.fori_loop(0, o_ref.shape[1] // LN_ROWS, norm_rows, 0)


def _out(m, w_out, x, gate, ln_g, ln_b, alpha):
    bsz, t, d = x.shape
    tm = _tile(t, 512)
    tn = _tile(d, 1024)
    return pl.pallas_call(
        functools.partial(_out_kernel, alpha=alpha, tn=tn),
        out_shape=jax.ShapeDtypeStruct((bsz, t, d), F32),
        grid=(bsz, t // tm, d // tn),
        in_specs=[pl.BlockSpec((1, tm, d), lambda b, i, j: (b, i, 0)),
                  pl.BlockSpec((d, tn), lambda b, i, j: (0, j)),
                  pl.BlockSpec((1, tm, tn), lambda b, i, j: (b, i, j)),
                  pl.BlockSpec((1, 1, tn), lambda b, i, j: (b, 0, j)),
                  pl.BlockSpec((1, d), lambda b, i, j: (0, 0)),
                  pl.BlockSpec((1, d), lambda b, i, j: (0, 0))],
        out_specs=pl.BlockSpec((1, tm, d), lambda b, i, j: (b, i, 0)),
        compiler_params=_params(("arbitrary", "arbitrary", "arbitrary")),
        name="out",
    )(m, w_out, x, gate, ln_g, ln_b)


def _layer(x, ctx, cond, w_ada, b_ada, w_in, rpb, lb_f, lb_b, norm_w, w_pa, w_pb, w_out, ln_g, ln_b, alpha):
    bsz, t, d = x.shape
    nh = (d // 2) // HEAD
    grp = d // 2
    tn = _tile(grp, 1024)
    per = grp // tn

    mod = _adaln(cond, w_ada, b_ada.reshape(1, -1))
    shift, scale, gate = (mod[:3, k * d:(k + 1) * d].reshape(3, 1, d) for k in range(3))
    ctx_row = bsz

    h_lat = _ln_mod(x, shift, scale, lambda b: b)
    h_ctx = _ln_mod(ctx, shift, scale, lambda b: ctx_row)

    w_in16 = w_in.astype(BF16)
    proj = _proj(h_lat, w_in16, 13 * per, lambda j: j, tn, "proj_lat")
    proj_ctx = _proj(h_ctx, w_in16, 5 * per, lambda j: jnp.where(j < 2 * per, j + per, j + 3 * per), tn,
                     "proj_ctx")

    rows = t // GRID_W
    y_a = _na(proj, proj_ctx, rpb, nh)
    y_b = _hgrn2(proj, proj_ctx, lb_f, lb_b, norm_w.reshape(1, HEAD), nh)
    m = _merge(y_a, y_b, w_pa.astype(BF16), w_pb.astype(BF16), proj, nh)
    return _out(m, w_out.astype(BF16), x, gate, ln_g.reshape(1, d), ln_b.reshape(1, d), alpha)


def kernel(x, c, ctx, c_ctx, w_ada, b_ada, w_in, na_rpb, hg_lb_fwd, hg_lb_bwd, hg_norm_w, w_pa, w_pb, w_out,
           ln_g, ln_b):
    depth = w_ada.shape[0]
    assert depth == 1, "the context stream update of deeper stacks is not implemented"
    bsz, _, d = x.shape
    alpha = (2.0 * depth) ** 0.25
    cond = jnp.concatenate([c, c_ctx[None], jnp.zeros((8 - bsz - 1, d), c.dtype)], axis=0)
    return _layer(x, ctx, cond, w_ada[0], b_ada[0], w_in[0], na_rpb[0], hg_lb_fwd, hg_lb_bwd, hg_norm_w[0],
                  w_pa[0], w_pb[0], w_out[0], ln_g[0], ln_b[0], alpha)
```

```python
import functools
import math

import jax
import jax.numpy as jnp
from jax import lax
from jax.experimental import pallas as pl
from jax.experimental.pallas import tpu as pltpu

F32 = jnp.float32
BF16 = jnp.bfloat16

GRID_W = 64
HEAD = 128
NA_KH = 8
NA_KW = 16
NA_QROWS = 4
NA_KROWS = 12
NA_UNROLL = 2
CHUNK = 64
HG_CHUNKS = 8
HG_TILE = HG_CHUNKS * CHUNK
LOG2E = math.log2(math.e)
LN_EPS = 1e-6
RMS_EPS = 1e-6
LN_ROWS = 64
VMEM_LIMIT = 56 * 1024 * 1024


def _params(semantics, vmem=VMEM_LIMIT):
    return pltpu.CompilerParams(dimension_semantics=semantics, vmem_limit_bytes=vmem)


def _tile(n, pref):
    t = min(n, pref)
    assert n % t == 0, (n, t)
    return t


def _adaln_kernel(c_ref, w_ref, b_ref, o_ref):
    c = c_ref[...]
    s = c * jax.nn.sigmoid(c)
    s_hi = s.astype(BF16)
    s_lo = (s - s_hi.astype(F32)).astype(BF16)
    w = w_ref[...]
    w_hi = w.astype(BF16)
    w_lo = (w - w_hi.astype(F32)).astype(BF16)
    n = s.shape[0]
    lhs = jnp.concatenate([s_hi, s_lo], axis=0)
    a = jnp.dot(lhs, w_hi, preferred_element_type=F32)
    b = jnp.dot(s_hi, w_lo, preferred_element_type=F32)
    o_ref[...] = a[:n] + a[n:] + b + b_ref[...]


def _adaln(cond, w, b):
    d, n = w.shape
    tn = _tile(n, 512)
    return pl.pallas_call(
        _adaln_kernel,
        out_shape=jax.ShapeDtypeStruct((cond.shape[0], n), F32),
        grid=(n // tn,),
        in_specs=[pl.BlockSpec(cond.shape, lambda j: (0, 0)),
                  pl.BlockSpec((d, tn), lambda j: (0, j)),
                  pl.BlockSpec((1, tn), lambda j: (0, j))],
        out_specs=pl.BlockSpec((cond.shape[0], tn), lambda j: (0, j)),
        compiler_params=_params(("arbitrary",)),
        name="adaln",
    )(cond, w, b)


def _ln_mod_kernel(x_ref, shift_ref, scale_ref, o_ref):
    x = x_ref[0]
    mu = jnp.mean(x, axis=-1, keepdims=True)
    xc = x - mu
    var = jnp.mean(xc * xc, axis=-1, keepdims=True)
    y = xc * lax.rsqrt(var + LN_EPS)
    o_ref[0] = (y * (1.0 + scale_ref[0]) + shift_ref[0]).astype(BF16)


def _ln_mod(x, shift, scale, cond_row):
    bsz, t, d = x.shape
    tm = _tile(t, 512)
    cmap = lambda b, i: (cond_row(b), 0, 0)
    return pl.pallas_call(
        _ln_mod_kernel,
        out_shape=jax.ShapeDtypeStruct((bsz, t, d), BF16),
        grid=(bsz, t // tm),
        in_specs=[pl.BlockSpec((1, tm, d), lambda b, i: (b, i, 0)),
                  pl.BlockSpec((1, 1, d), cmap),
                  pl.BlockSpec((1, 1, d), cmap)],
        out_specs=pl.BlockSpec((1, tm, d), lambda b, i: (b, i, 0)),
        compiler_params=_params(("arbitrary", "arbitrary")),
        name="ln_mod",
    )(x, shift, scale)


def _store_head_major(o_ref, acc):
    for s in range(o_ref.shape[1]):
        o_ref[0, s] = acc[:, s * HEAD:(s + 1) * HEAD].astype(BF16)


def _proj_kernel(h_ref, w_ref, *refs):
    _store_head_major(refs[-1], jnp.dot(h_ref[0], w_ref[...], preferred_element_type=F32))


def _proj_cast_kernel(h_ref, w_ref, *refs):
    n_extra = (len(refs) - 2) // 2
    o_ref, w16_ref = refs[n_extra], refs[n_extra + 1]
    w16 = w_ref[...].astype(BF16)
    w16_ref[...] = w16
    _store_head_major(o_ref, jnp.dot(h_ref[0], w16, preferred_element_type=F32))
    for src, dst in zip(refs[:n_extra], refs[n_extra + 2:]):
        dst[...] = src[...].astype(BF16)


def _proj_first(h, w, later_weights):
    bsz, t, d = h.shape
    n = w.shape[1]
    tm = _tile(t, 1024)
    tn = next(c for c in (512, 256, HEAD) if n % c == 0)
    sub = tn // HEAD
    steps = n // tn
    in_specs = [pl.BlockSpec((1, tm, d), lambda j: (0, 0, 0)), pl.BlockSpec((d, tn), lambda j: (0, j))]
    out_specs = [pl.BlockSpec((1, sub, tm, HEAD), lambda j: (0, j, 0, 0)), pl.BlockSpec((d, tn), lambda j: (0, j))]
    out_shape = [jax.ShapeDtypeStruct((bsz, n // HEAD, t, HEAD), BF16), jax.ShapeDtypeStruct((d, n), BF16)]
    for lw in later_weights:
        rows, cols = lw.shape
        split = next(s for s in range(steps, 0, -1) if rows % s == 0 and (rows // s) % 16 == 0)
        spec = pl.BlockSpec((rows // split, cols), lambda j, split=split: (jnp.minimum(j, split - 1), 0))
        in_specs.append(spec)
        out_specs.append(spec)
        out_shape.append(jax.ShapeDtypeStruct(lw.shape, BF16))
    return pl.pallas_call(
        _proj_cast_kernel,
        out_shape=out_shape,
        grid=(steps,),
        in_specs=in_specs,
        out_specs=out_specs,
        compiler_params=_params(("arbitrary",)),
        name="proj_first",
    )(h, w, *later_weights)


def _proj(h, w, n_tiles, col_tile, tn, name, into=None, skip_blocks=0):
    bsz, t, d = h.shape
    tm = _tile(t, 1024)
    nb = t // tm
    sub = tn // HEAD
    row = lambda r: ((r + skip_blocks) // nb, (r + skip_blocks) % nb)
    in_specs = [pl.BlockSpec((1, tm, d), lambda r, j: (*row(r), 0)),
                pl.BlockSpec((d, tn), lambda r, j: (0, col_tile(j)))]
    args = [h, w]
    if into is not None:
        in_specs.append(pl.BlockSpec(memory_space=pl.ANY))
        args.append(into)
    return pl.pallas_call(
        _proj_kernel,
        out_shape=jax.ShapeDtypeStruct((bsz, n_tiles * sub, t, HEAD), BF16),
        grid=(bsz * nb - skip_blocks, n_tiles),
        in_specs=in_specs,
        out_specs=pl.BlockSpec((1, sub, tm, HEAD), lambda r, j: (row(r)[0], j, row(r)[1], 0)),
        input_output_aliases={} if into is None else {2: 0},
        compiler_params=_params(("arbitrary", "arbitrary")),
        name=name,
    )(*args)


N_DROW = 2 * NA_KH - 1
N_DCOL = 2 * NA_KW - 1


def _na_block_kinds(rows):
    return ((0, 0), (NA_QROWS, 0), (rows - NA_QROWS, rows - NA_KROWS))


def _build_bias(rpb_ref, head, band_s, bias_s, rows):
    shape = (GRID_W, 2 * GRID_W)
    qc = lax.broadcasted_iota(jnp.int32, shape, 0)
    lane = lax.broadcasted_iota(jnp.int32, shape, 1)
    kc = lane & (GRID_W - 1)
    d_col = jnp.clip(kc - qc, -(NA_KW - 1), NA_KW - 1) + (NA_KW - 1)
    col_start = jnp.clip(qc - NA_KW // 2, 0, GRID_W - NA_KW)
    col_off = kc - col_start
    neg = jnp.full(shape, -jnp.inf, F32)
    base = head * (N_DROW * N_DCOL)
    for d in range(N_DROW):
        acc = neg
        for j in range(N_DCOL):
            acc = jnp.where(d_col == j, rpb_ref[base + d * N_DCOL + j] * LOG2E, acc)
        band_s[d] = jnp.where(col_off < 0, neg, jnp.where(col_off < NA_KW, acc, neg))
    left = lane < GRID_W
    for kind, (r0, kb) in enumerate(_na_block_kinds(rows)):
        for qi in range(NA_QROWS):
            r = r0 + qi
            row_start = min(max(r - NA_KH // 2, 0), rows - NA_KH)

            def band(kr):
                return band_s[kr - r + NA_KH - 1] if row_start <= kr < row_start + NA_KH else neg

            for jp in range(NA_KROWS // 2):
                kr = kb + 2 * jp
                bias_s[kind, qi * GRID_W:(qi + 1) * GRID_W, jp * 2 * GRID_W:(jp + 1) * 2 * GRID_W] = (
                    jnp.where(left, band(kr), band(kr + 1)))


def _na_kernel(rpb_ref, q_ref, k_ref, v_ref, z_ref, kc_ref, vc_ref, o_ref, band_s, bias_s, *, rows):
    nq = NA_QROWS * GRID_W
    nk = NA_KROWS * GRID_W
    n_blocks = rows // NA_QROWS
    scale = HEAD ** -0.5 * LOG2E
    kc = kc_ref[0, 0]
    vc = vc_ref[0, 0]
    nt = (((1,), (1,)), ((), ()))

    @pl.when(pl.program_id(1) == 0)
    def _():
        _build_bias(rpb_ref, pl.program_id(0), band_s, bias_s, rows)

    def scores(i):
        r0 = i * NA_QROWS
        kb = jnp.clip(r0 - NA_KH // 2, 0, rows - NA_KROWS)
        kind = jnp.where(i == 0, 0, jnp.where(i == n_blocks - 1, 2, 1))
        q_off = pl.multiple_of(r0 * GRID_W, GRID_W)
        k_off = pl.multiple_of(kb * GRID_W, GRID_W)
        q = (q_ref[0, 0, pl.ds(q_off, nq), :].astype(F32) * scale).astype(BF16)
        s_loc = lax.dot_general(q, k_ref[0, 0, pl.ds(k_off, nk), :], nt, preferred_element_type=F32)
        s_ctx = lax.dot_general(q, kc, nt, preferred_element_type=F32)
        return q_off, k_off, s_loc + bias_s[kind], s_ctx

    def attend(q_off, k_off, s_loc, s_ctx):
        m = jnp.maximum(jnp.max(s_loc, axis=-1, keepdims=True), jnp.max(s_ctx, axis=-1, keepdims=True))
        p_loc = jnp.exp2(s_loc - m)
        p_ctx = jnp.exp2(s_ctx - m)
        denom = jnp.sum(p_loc, axis=-1, keepdims=True) + jnp.sum(p_ctx, axis=-1, keepdims=True)
        o = (jnp.dot(p_loc.astype(BF16), v_ref[0, 0, pl.ds(k_off, nk), :], preferred_element_type=F32)
             + jnp.dot(p_ctx.astype(BF16), vc, preferred_element_type=F32))
        o = o / denom
        z = z_ref[0, 0, pl.ds(q_off, nq), :].astype(F32)
        o_ref[0, 0, pl.ds(q_off, nq), :] = (o * (z * jax.nn.sigmoid(z))).astype(BF16)

    def blocks(n, carry):
        staged = [scores(n * NA_UNROLL + u) for u in range(NA_UNROLL)]
        for args in staged:
            attend(*args)
        return carry

    lax.fori_loop(0, n_blocks // NA_UNROLL, blocks, 0)


def _na(proj, proj_ctx, rpb, nh):
    bsz, _, t, _ = proj.shape
    lc = proj_ctx.shape[2]
    rows = t // GRID_W
    assert rows % (NA_QROWS * NA_UNROLL) == 0 and rows >= NA_KROWS + NA_QROWS
    assert rpb.shape == (nh, N_DROW, N_DCOL)
    lat = lambda g: pl.BlockSpec((1, 1, t, HEAD), lambda h, b: (b, g * nh + h, 0, 0))
    ctx = lambda g: pl.BlockSpec((1, 1, lc, HEAD), lambda h, b: (g * nh + h, b, 0, 0))
    return pl.pallas_call(
        functools.partial(_na_kernel, rows=rows),
        out_shape=jax.ShapeDtypeStruct((bsz, nh, t, HEAD), BF16),
        grid=(nh, bsz),
        in_specs=[pl.BlockSpec(memory_space=pltpu.SMEM),
                  lat(0), lat(1), lat(2), lat(3), ctx(0), ctx(1)],
        out_specs=pl.BlockSpec((1, 1, t, HEAD), lambda h, b: (b, h, 0, 0)),
        scratch_shapes=[pltpu.VMEM((N_DROW, GRID_W, 2 * GRID_W), F32),
                        pltpu.VMEM((3, NA_QROWS * GRID_W, NA_KROWS * GRID_W), F32)],
        compiler_params=_params(("arbitrary", "arbitrary")),
        name="na",
    )(rpb.astype(F32).reshape(-1), proj, proj, proj, proj, proj_ctx, proj_ctx)


def _seg_cumsum(x, reverse):
    r_i = lax.broadcasted_iota(jnp.int32, (CHUNK, 2 * CHUNK), 0)
    c_i = lax.broadcasted_iota(jnp.int32, (CHUNK, 2 * CHUNK), 1) & (CHUNK - 1)
    tri2 = jnp.where((r_i <= c_i) if reverse else (r_i >= c_i), 1.0, 0.0).astype(BF16)
    hi = x.astype(BF16)
    lo = (x - hi.astype(F32)).astype(BF16)
    out = []
    for c in range(x.shape[0] // CHUNK):
        rows = slice(c * CHUNK, (c + 1) * CHUNK)
        out.append(jnp.dot(tri2, jnp.concatenate([hi[rows], lo[rows]], axis=0), preferred_element_type=F32))
    return jnp.concatenate(out, axis=0)


def _forget(f_pre, lb):
    sg = jax.nn.sigmoid(f_pre)
    one_m_lb = 1.0 - lb
    return one_m_lb * (1.0 - sg), jnp.log(lb + one_m_lb * sg)


def _lower_bound(logits):
    m = jnp.max(logits, axis=0, keepdims=True)
    e = jnp.exp(logits - m)
    return e[0:1] / jnp.sum(e, axis=0, keepdims=True)


def _gates(f_pre, lb, reverse):
    kk, logf = _forget(f_pre, lb)
    cum = _seg_cumsum(logf, reverse).reshape(-1, CHUNK, HEAD)
    mid = CHUNK // 2
    ref = cum[:, mid:mid + 1] if reverse else cum[:, mid - 1:mid]
    last = cum[:, 0:1] if reverse else cum[:, CHUNK - 1:CHUNK]
    return kk.reshape(cum.shape), cum, ref, last


def _ctx_state(f_pre, v, lb, reverse):
    kk, cum, _, last = _gates(f_pre, lb, reverse)
    kl = (kk * jnp.exp(last - cum)).astype(BF16)
    dec = jnp.exp(last)
    tn = (((0,), (0,)), ((), ()))
    n_chunks = kl.shape[0]
    st = jnp.zeros((HEAD, HEAD), F32)
    for c in (reversed(range(n_chunks)) if reverse else range(n_chunks)):
        u_t = lax.dot_general(v[c * CHUNK:(c + 1) * CHUNK], kl[c], tn, preferred_element_type=F32)
        st = st * dec[c] + u_t
    return st


def _hg_prepare(qs, f_pre, lb, reverse):
    kk, cum, ref, last = _gates(f_pre, lb, reverse)
    e = jnp.exp(cum - ref)
    qd = qs.reshape(cum.shape) * e
    kd = kk / e
    qe = qd * jnp.exp(ref)
    kl = kd * jnp.exp(last - ref)
    flat = lambda a: a.reshape(-1, HEAD).astype(BF16)
    return flat(qd), flat(kd), flat(qe), flat(kl), jnp.exp(last).reshape(-1, HEAD)


def _hg_kernel(q_ref, ff_ref, fb_ref, i_ref, g_ref, ffc_ref, fbc_ref, ic_ref, lbf_ref, lbb_ref, nw_ref,
               o_ref, qs_s, qd_s, kd_s, qe_s, kl_s, dec_s, ob_s, *, t):
    n_tiles = t // HG_TILE
    lb_f = _lower_bound(lbf_ref[...])
    lb_b = _lower_bound(lbb_ref[...])
    nw = nw_ref[...]
    r_i = lax.broadcasted_iota(jnp.int32, (CHUNK, CHUNK), 0)
    c_i = lax.broadcasted_iota(jnp.int32, (CHUNK, CHUNK), 1)
    nt = (((1,), (1,)), ((), ()))
    tn = (((0,), (0,)), ((), ()))

    def prepare(tix, f_ref, lb, reverse, first):
        rs = pl.ds(pl.multiple_of(tix * HG_TILE, HG_TILE), HG_TILE)
        if first:
            q = q_ref[0, 0, rs, :].astype(F32)
            qs = q * jax.nn.sigmoid(q)
        else:
            qs = qs_s[rs, :]
        return (qs if first else None,) + _hg_prepare(qs, f_ref[0, 0, rs, :].astype(F32), lb, reverse)

    def stage(tix, vals):
        rs = pl.ds(pl.multiple_of(tix * HG_TILE, HG_TILE), HG_TILE)
        qs, qd, kd, qe, kl, dec = vals
        if qs is not None:
            qs_s[rs, :] = qs
        qd_s[rs, :] = qd
        kd_s[rs, :] = kd
        qe_s[rs, :] = qe
        kl_s[rs, :] = kl
        dec_s[tix] = dec

    def scan(tix, st, reverse, emit):
        tri = (r_i <= c_i) if reverse else (r_i >= c_i)
        order = [HG_CHUNKS - 1 - u if reverse else u for u in range(HG_CHUNKS)]
        cs = {j: pl.ds(pl.multiple_of(tix * HG_TILE + j * CHUNK, CHUNK), CHUNK) for j in order}
        v = {j: i_ref[0, 0, cs[j], :] for j in order}
        a = {j: lax.dot_general(qd_s[cs[j], :], kd_s[cs[j], :], nt, preferred_element_type=F32) for j in order}
        u_t = {j: lax.dot_general(v[j], kl_s[cs[j], :], tn, preferred_element_type=F32) for j in order}
        dec = dec_s[tix]
        for j in order:
            o = (jnp.dot(jnp.where(tri, a[j], 0.0).astype(BF16), v[j], preferred_element_type=F32)
                 + lax.dot_general(qe_s[cs[j], :], st.astype(BF16), nt, preferred_element_type=F32))
            emit(cs[j], o)
            st = st * dec[j:j + 1] + u_t[j]
        return st

    def direction(f_ref, fc_ref, lb, reverse, first, emit):
        tile_of = (lambda n: n_tiles - 1 - n) if reverse else (lambda n: n)
        stage(tile_of(0), prepare(tile_of(0), f_ref, lb, reverse, first))
        st = _ctx_state(fc_ref[0, 0].astype(F32), ic_ref[0, 0], lb, reverse)

        def trip(n, st):
            nxt = tile_of(jnp.minimum(n + 1, n_tiles - 1))
            vals = prepare(nxt, f_ref, lb, reverse, first)
            st = scan(tile_of(n), st, reverse, emit)
            stage(nxt, vals)
            return st

        lax.fori_loop(0, n_tiles, trip, st)

    def emit_backward(cs, o):
        ob_s[cs, :] = o

    def emit_forward(cs, o):
        o = o + ob_s[cs, :]
        o = o * lax.rsqrt(jnp.mean(o * o, axis=-1, keepdims=True) + RMS_EPS) * nw
        g = g_ref[0, 0, cs, :].astype(F32)
        o_ref[0, 0, cs, :] = (o * (g * jax.nn.sigmoid(g))).astype(BF16)

    direction(fb_ref, fbc_ref, lb_b, True, True, emit_backward)
    direction(ff_ref, ffc_ref, lb_f, False, False, emit_forward)


def _hgrn2(proj, proj_ctx, lb_f, lb_b, norm_w, nh):
    bsz, _, t, _ = proj.shape
    lc = proj_ctx.shape[2]
    n_slots = lb_f.shape[0]
    assert t % HG_TILE == 0 and lc % CHUNK == 0
    lat = lambda g: pl.BlockSpec((1, 1, t, HEAD), lambda b, h: (b, g * nh + h, 0, 0))
    ctx = lambda g: pl.BlockSpec((1, 1, lc, HEAD), lambda b, h: (g * nh + h, b, 0, 0))
    lbs = pl.BlockSpec((n_slots, HEAD), lambda b, h: (0, h))
    staged = pltpu.VMEM((t, HEAD), BF16)
    return pl.pallas_call(
        functools.partial(_hg_kernel, t=t),
        out_shape=jax.ShapeDtypeStruct((bsz, nh, t, HEAD), BF16),
        grid=(bsz, nh),
        in_specs=[lat(4), lat(5), lat(6), lat(7), lat(8), ctx(2), ctx(3), ctx(4), lbs, lbs,
                  pl.BlockSpec((1, HEAD), lambda b, h: (0, 0))],
        out_specs=pl.BlockSpec((1, 1, t, HEAD), lambda b, h: (b, h, 0, 0)),
        scratch_shapes=[pltpu.VMEM((t, HEAD), F32), staged, staged, staged, staged,
                        pltpu.VMEM((t // HG_TILE, HG_CHUNKS, HEAD), F32), pltpu.VMEM((t, HEAD), F32)],
        compiler_params=_params(("arbitrary", "arbitrary")),
        name="hgrn2",
    )(proj, proj, proj, proj, proj, proj_ctx, proj_ctx, proj_ctx, lb_f, lb_b, norm_w)


def _merge_kernel(ya_ref, yb_ref, wa_ref, wb_ref, ga_ref, gb_ref, o_ref, ya_s, yb_s):
    @pl.when(pl.program_id(2) == 0)
    def _():
        for h in range(ya_ref.shape[1]):
            ya_s[:, h * HEAD:(h + 1) * HEAD] = ya_ref[0, h]
            yb_s[:, h * HEAD:(h + 1) * HEAD] = yb_ref[0, h]

    ta = jnp.dot(ya_s[...], wa_ref[...], preferred_element_type=F32)
    tb = jnp.dot(yb_s[...], wb_ref[...], preferred_element_type=F32)
    for s in range(ga_ref.shape[1]):
        cs = slice(s * HEAD, (s + 1) * HEAD)
        m = (jax.nn.sigmoid(ga_ref[0, s].astype(F32)) * ta[:, cs]
             + jax.nn.sigmoid(gb_ref[0, s].astype(F32)) * tb[:, cs])
        o_ref[0, :, cs] = m.astype(BF16)


def _merge(ya, yb, w_pa, w_pb, proj, nh):
    bsz, _, t, _ = ya.shape
    d = w_pa.shape[1]
    tm = _tile(t, 512)
    tn = _tile(nh * HEAD, 1024)
    sub = tn // HEAD
    ga0 = 9 * nh // sub
    gb0 = 11 * nh // sub
    assert (9 * nh) % sub == 0 and (11 * nh) % sub == 0
    yspec = pl.BlockSpec((1, nh, tm, HEAD), lambda b, i, j: (b, 0, i, 0))
    wspec = pl.BlockSpec((nh * HEAD, tn), lambda b, i, j: (0, j))
    return pl.pallas_call(
        _merge_kernel,
        out_shape=jax.ShapeDtypeStruct((bsz, t, d), BF16),
        grid=(bsz, t // tm, d // tn),
        in_specs=[yspec, yspec, wspec, wspec,
                  pl.BlockSpec((1, sub, tm, HEAD), lambda b, i, j: (b, ga0 + j, i, 0)),
                  pl.BlockSpec((1, sub, tm, HEAD), lambda b, i, j: (b, gb0 + j, i, 0))],
        out_specs=pl.BlockSpec((1, tm, tn), lambda b, i, j: (b, i, j)),
        scratch_shapes=[pltpu.VMEM((tm, nh * HEAD), BF16), pltpu.VMEM((tm, nh * HEAD), BF16)],
        compiler_params=_params(("arbitrary", "arbitrary", "arbitrary")),
        name="merge",
    )(ya, yb, w_pa, w_pb, proj, proj)


def _out_kernel(m_ref, w_ref, x_ref, gate_ref, lng_ref, lnb_ref, o_ref, *, alpha, tn):
    j = pl.program_id(2)
    out = jnp.dot(m_ref[0], w_ref[...], preferred_element_type=F32)
    col = pl.multiple_of(j * tn, tn)
    o_ref[0, :, pl.ds(col, tn)] = alpha * x_ref[0] + gate_ref[0] * out

    @pl.when(j == pl.num_programs(2) - 1)
    def _():
        def norm_rows(i, carry):
            rs = pl.ds(pl.multiple_of(i * LN_ROWS, LN_ROWS), LN_ROWS)
            r = o_ref[0, rs, :]
            mu = jnp.mean(r, axis=-1, keepdims=True)
            rc = r - mu
            var = jnp.mean(rc * rc, axis=-1, keepdims=True)
            o_ref[0, rs, :] = rc * lax.rsqrt(var + LN_EPS) * lng_ref[...] + lnb_ref[...]
            return carry

        lax.fori_loop(0, o_ref.shape[1] // LN_ROWS, norm_rows, 0)


def _out(m, w_out, x, gate, ln_g, ln_b, alpha):
    bsz, t, d = x.shape
    tm = _tile(t, 512)
    tn = _tile(d, 1024)
    return pl.pallas_call(
        functools.partial(_out_kernel, alpha=alpha, tn=tn),
        out_shape=jax.ShapeDtypeStruct((bsz, t, d), F32),
        grid=(bsz, t // tm, d // tn),
        in_specs=[pl.BlockSpec((1, tm, d), lambda b, i, j: (b, i, 0)),
                  pl.BlockSpec((d, tn), lambda b, i, j: (0, j)),
                  pl.BlockSpec((1, tm, tn), lambda b, i, j: (b, i, j)),
                  pl.BlockSpec((1, 1, tn), lambda b, i, j: (b, 0, j)),
                  pl.BlockSpec((1, d), lambda b, i, j: (0, 0)),
                  pl.BlockSpec((1, d), lambda b, i, j: (0, 0))],
        out_specs=pl.BlockSpec((1, tm, d), lambda b, i, j: (b, i, 0)),
        compiler_params=_params(("arbitrary", "arbitrary", "arbitrary")),
        name="out",
    )(m, w_out, x, gate, ln_g, ln_b)


def _layer(x, ctx, cond, w_ada, b_ada, w_in, rpb, lb_f, lb_b, norm_w, w_pa, w_pb, w_out, ln_g, ln_b, alpha):
    bsz, t, d = x.shape
    nh = (d // 2) // HEAD
    grp = d // 2
    tn = _tile(grp, 1024)
    per = grp // tn

    mod = _adaln(cond, w_ada, b_ada.reshape(1, -1))
    shift, scale, gate = (mod[:3, k * d:(k + 1) * d].reshape(3, 1, d) for k in range(3))
    ctx_row = bsz

    h_lat = _ln_mod(x, shift, scale, lambda b: b)
    h_ctx = _ln_mod(ctx, shift, scale, lambda b: ctx_row)

    first, w_in16, w_pa16, w_pb16, w_out16 = _proj_first(h_lat, w_in, (w_pa, w_pb, w_out))
    proj = _proj(h_lat, w_in16, 13 * per, lambda j: j, tn, "proj_lat", into=first, skip_blocks=1)
    proj_ctx = _proj(h_ctx.reshape(1, -1, d), w_in16, 5 * per,
                     lambda j: jnp.where(j < 2 * per, j + per, j + 3 * per), tn, "proj_ctx")
    proj_ctx = proj_ctx.reshape(5 * nh, bsz, ctx.shape[1], HEAD)

    rows = t // GRID_W
    y_a = _na(proj, proj_ctx, rpb, nh)
    y_b = _hgrn2(proj, proj_ctx, lb_f, lb_b, norm_w.reshape(1, HEAD), nh)
    m = _merge(y_a, y_b, w_pa16, w_pb16, proj, nh)
    return _out(m, w_out16, x, gate, ln_g.reshape(1, d), ln_b.reshape(1, d), alpha)


def kernel(x, c, ctx, c_ctx, w_ada, b_ada, w_in, na_rpb, hg_lb_fwd, hg_lb_bwd, hg_norm_w, w_pa, w_pb, w_out,
           ln_g, ln_b):
    depth = w_ada.shape[0]
    assert depth == 1, "the context stream update of deeper stacks is not implemented"
    bsz, _, d = x.shape
    alpha = (2.0 * depth) ** 0.25
    cond = jnp.concatenate([c, c_ctx[None], jnp.zeros((8 - bsz - 1, d), c.dtype)], axis=0)
    return _layer(x, ctx, cond, w_ada[0], b_ada[0], w_in[0], na_rpb[0], hg_lb_fwd, hg_lb_bwd, hg_norm_w[0],
                  w_pa[0], w_pb[0], w_out[0], ln_g[0], ln_b[0], alpha)
```

```python
import functools
import math

import jax
import jax.numpy as jnp
from jax import lax
from jax.experimental import pallas as pl
from jax.experimental.pallas import tpu as pltpu

F32 = jnp.float32
BF16 = jnp.bfloat16

GRID_W = 64
HEAD = 128
NA_KH = 8
NA_KW = 16
NA_QROWS = 4
NA_KROWS = 12
NA_UNROLL = 4
CHUNK = 64
HG_CHUNKS = 16
HG_TILE = HG_CHUNKS * CHUNK
LN_ROWS = 64
LOG2E = math.log2(math.e)
LN_EPS = 1e-6
RMS_EPS = 1e-6
VMEM_LIMIT = 56 * 1024 * 1024


def _params(semantics, vmem=VMEM_LIMIT):
    return pltpu.CompilerParams(dimension_semantics=semantics, vmem_limit_bytes=vmem)


def _tile(n, pref):
    t = min(n, pref)
    assert n % t == 0, (n, t)
    return t


def _adaln_kernel(c_ref, w_ref, b_ref, o_ref):
    c = c_ref[...]
    s = c * jax.nn.sigmoid(c)
    s_hi = s.astype(BF16)
    s_lo = (s - s_hi.astype(F32)).astype(BF16)
    w = w_ref[...]
    w_hi = w.astype(BF16)
    w_lo = (w - w_hi.astype(F32)).astype(BF16)
    n = s.shape[0]
    lhs = jnp.concatenate([s_hi, s_lo], axis=0)
    a = jnp.dot(lhs, w_hi, preferred_element_type=F32)
    b = jnp.dot(s_hi, w_lo, preferred_element_type=F32)
    o_ref[...] = a[:n] + a[n:] + b + b_ref[...]


def _adaln(cond, w, b):
    d, n = w.shape
    tn = _tile(n, 512)
    return pl.pallas_call(
        _adaln_kernel,
        out_shape=jax.ShapeDtypeStruct((cond.shape[0], n), F32),
        grid=(n // tn,),
        in_specs=[pl.BlockSpec(cond.shape, lambda j: (0, 0)),
                  pl.BlockSpec((d, tn), lambda j: (0, j)),
                  pl.BlockSpec((1, tn), lambda j: (0, j))],
        out_specs=pl.BlockSpec((cond.shape[0], tn), lambda j: (0, j)),
        compiler_params=_params(("arbitrary",)),
        name="adaln",
    )(cond, w, b)


def _ln_mod_kernel(x_ref, shift_ref, scale_ref, o_ref):
    x = x_ref[0]
    mu = jnp.mean(x, axis=-1, keepdims=True)
    xc = x - mu
    var = jnp.mean(xc * xc, axis=-1, keepdims=True)
    y = xc * lax.rsqrt(var + LN_EPS)
    o_ref[0] = (y * (1.0 + scale_ref[0]) + shift_ref[0]).astype(BF16)


def _ln_mod(x, shift, scale, cond_row):
    bsz, t, d = x.shape
    tm = _tile(t, 512)
    cmap = lambda b, i: (cond_row(b), 0, 0)
    return pl.pallas_call(
        _ln_mod_kernel,
        out_shape=jax.ShapeDtypeStruct((bsz, t, d), BF16),
        grid=(bsz, t // tm),
        in_specs=[pl.BlockSpec((1, tm, d), lambda b, i: (b, i, 0)),
                  pl.BlockSpec((1, 1, d), cmap),
                  pl.BlockSpec((1, 1, d), cmap)],
        out_specs=pl.BlockSpec((1, tm, d), lambda b, i: (b, i, 0)),
        compiler_params=_params(("arbitrary", "arbitrary")),
        name="ln_mod",
    )(x, shift, scale)


def _store_head_major(o_ref, acc):
    for s in range(o_ref.shape[1]):
        o_ref[0, s] = acc[:, s * HEAD:(s + 1) * HEAD].astype(BF16)


def _proj_kernel(h_ref, w_ref, *refs, n_cast, aliased):
    refs = refs[1:] if aliased else refs
    _store_head_major(refs[n_cast], jnp.dot(h_ref[0], w_ref[...], preferred_element_type=F32))
    for src, dst in zip(refs[:n_cast], refs[n_cast + 1:]):
        dst[...] = src[...].astype(BF16)


def _proj_cast_kernel(h_ref, w_ref, o_ref, w16_ref):
    w16 = w_ref[...].astype(BF16)
    w16_ref[...] = w16
    _store_head_major(o_ref, jnp.dot(h_ref[0], w16, preferred_element_type=F32))


def _proj_first(h, w):
    bsz, t, d = h.shape
    n = w.shape[1]
    tm = _tile(t, 1024)
    tn = next(c for c in (512, 256, HEAD) if n % c == 0)
    sub = tn // HEAD
    return pl.pallas_call(
        _proj_cast_kernel,
        out_shape=(jax.ShapeDtypeStruct((bsz, n // HEAD, t, HEAD), BF16), jax.ShapeDtypeStruct((d, n), BF16)),
        grid=(n // tn,),
        in_specs=[pl.BlockSpec((1, tm, d), lambda j: (0, 0, 0)),
                  pl.BlockSpec((d, tn), lambda j: (0, j))],
        out_specs=[pl.BlockSpec((1, sub, tm, HEAD), lambda j: (0, j, 0, 0)),
                   pl.BlockSpec((d, tn), lambda j: (0, j))],
        compiler_params=_params(("arbitrary",)),
        name="proj_first",
    )(h, w)


def _proj(h, w, n_tiles, col_tile, tn, name, into=None, skip_blocks=0, cast=()):
    bsz, t, d = h.shape
    tm = _tile(t, 1024)
    nb = t // tm
    sub = tn // HEAD
    n_rows = bsz * nb - skip_blocks
    row = lambda r: ((r + skip_blocks) // nb, (r + skip_blocks) % nb)
    in_specs = [pl.BlockSpec((1, tm, d), lambda r, j: (*row(r), 0)),
                pl.BlockSpec((d, tn), lambda r, j: (0, col_tile(j)))]
    args = [h, w]
    if into is not None:
        in_specs.append(pl.BlockSpec(memory_space=pl.ANY))
        args.append(into)
    out_specs = [pl.BlockSpec((1, sub, tm, HEAD), lambda r, j: (row(r)[0], j, row(r)[1], 0))]
    out_shape = [jax.ShapeDtypeStruct((bsz, n_tiles * sub, t, HEAD), BF16)]
    for cw in cast:
        rows, cols = cw.shape
        split = next(s for s in range(n_rows * n_tiles, 0, -1) if rows % s == 0 and (rows // s) % 16 == 0)
        spec = pl.BlockSpec((rows // split, cols),
                            lambda r, j, split=split: (jnp.minimum(r * n_tiles + j, split - 1), 0))
        in_specs.append(spec)
        out_specs.append(spec)
        out_shape.append(jax.ShapeDtypeStruct(cw.shape, BF16))
    res = pl.pallas_call(
        functools.partial(_proj_kernel, n_cast=len(cast), aliased=into is not None),
        out_shape=out_shape,
        grid=(n_rows, n_tiles),
        in_specs=in_specs,
        out_specs=out_specs,
        input_output_aliases={} if into is None else {2: 0},
        compiler_params=_params(("arbitrary", "arbitrary")),
        name=name,
    )(*args, *cast)
    return res if cast else res[0]


N_DROW = 2 * NA_KH - 1
N_DCOL = 2 * NA_KW - 1


def _na_block_kinds(rows):
    return ((0, 0), (NA_QROWS, 0), (rows - NA_QROWS, rows - NA_KROWS))


def _build_bias(rpb_ref, head, band_s, bias_s, rows):
    shape = (GRID_W, 2 * GRID_W)
    qc = lax.broadcasted_iota(jnp.int32, shape, 0)
    lane = lax.broadcasted_iota(jnp.int32, shape, 1)
    kc = lane & (GRID_W - 1)
    d_col = jnp.clip(kc - qc, -(NA_KW - 1), NA_KW - 1) + (NA_KW - 1)
    col_start = jnp.clip(qc - NA_KW // 2, 0, GRID_W - NA_KW)
    col_off = kc - col_start
    neg = jnp.full(shape, -jnp.inf, F32)
    base = head * (N_DROW * N_DCOL)
    for d in range(N_DROW):
        acc = neg
        for j in range(N_DCOL):
            acc = jnp.where(d_col == j, rpb_ref[base + d * N_DCOL + j] * LOG2E, acc)
        band_s[d] = jnp.where(col_off < 0, neg, jnp.where(col_off < NA_KW, acc, neg))
    left = lane < GRID_W
    for kind, (r0, kb) in enumerate(_na_block_kinds(rows)):
        for qi in range(NA_QROWS):
            r = r0 + qi
            row_start = min(max(r - NA_KH // 2, 0), rows - NA_KH)

            def band(kr):
                return band_s[kr - r + NA_KH - 1] if row_start <= kr < row_start + NA_KH else neg

            for jp in range(NA_KROWS // 2):
                kr = kb + 2 * jp
                bias_s[kind, qi * GRID_W:(qi + 1) * GRID_W, jp * 2 * GRID_W:(jp + 1) * 2 * GRID_W] = (
                    jnp.where(left, band(kr), band(kr + 1)))


def _na_kernel(rpb_ref, q_ref, k_ref, v_ref, z_ref, kc_ref, vc_ref, o_ref, band_s, bias_s, *, rows):
    nq = NA_QROWS * GRID_W
    nk = NA_KROWS * GRID_W
    n_blocks = rows // NA_QROWS
    scale = HEAD ** -0.5 * LOG2E
    kc = kc_ref[0, 0]
    vc = vc_ref[0, 0]
    nt = (((1,), (1,)), ((), ()))

    @pl.when(pl.program_id(1) == 0)
    def _():
        _build_bias(rpb_ref, pl.program_id(0), band_s, bias_s, rows)

    def scores(i):
        r0 = i * NA_QROWS
        kb = jnp.clip(r0 - NA_KH // 2, 0, rows - NA_KROWS)
        kind = jnp.where(i == 0, 0, jnp.where(i == n_blocks - 1, 2, 1))
        q_off = pl.multiple_of(r0 * GRID_W, GRID_W)
        k_off = pl.multiple_of(kb * GRID_W, GRID_W)
        q = (q_ref[0, 0, pl.ds(q_off, nq), :].astype(F32) * scale).astype(BF16)
        s_loc = lax.dot_general(q, k_ref[0, 0, pl.ds(k_off, nk), :], nt, preferred_element_type=F32)
        s_ctx = lax.dot_general(q, kc, nt, preferred_element_type=F32)
        return q_off, k_off, s_loc + bias_s[kind], s_ctx

    def attend(q_off, k_off, s_loc, s_ctx):
        m = jnp.maximum(jnp.max(s_loc, axis=-1, keepdims=True), jnp.max(s_ctx, axis=-1, keepdims=True))
        p_loc = jnp.exp2(s_loc - m)
        p_ctx = jnp.exp2(s_ctx - m)
        denom = jnp.sum(p_loc, axis=-1, keepdims=True) + jnp.sum(p_ctx, axis=-1, keepdims=True)
        o = (jnp.dot(p_loc.astype(BF16), v_ref[0, 0, pl.ds(k_off, nk), :], preferred_element_type=F32)
             + jnp.dot(p_ctx.astype(BF16), vc, preferred_element_type=F32))
        o = o / denom
        z = z_ref[0, 0, pl.ds(q_off, nq), :].astype(F32)
        o_ref[0, 0, pl.ds(q_off, nq), :] = (o * (z * jax.nn.sigmoid(z))).astype(BF16)

    def blocks(n, carry):
        staged = [scores(n * NA_UNROLL + u) for u in range(NA_UNROLL)]
        for args in staged:
            attend(*args)
        return carry

    lax.fori_loop(0, n_blocks // NA_UNROLL, blocks, 0)


def _na(proj, proj_ctx, rpb, nh):
    bsz, _, t, _ = proj.shape
    lc = proj_ctx.shape[2]
    rows = t // GRID_W
    assert rows % (NA_QROWS * NA_UNROLL) == 0 and rows >= NA_KROWS + NA_QROWS
    assert rpb.shape == (nh, N_DROW, N_DCOL)
    lat = lambda g: pl.BlockSpec((1, 1, t, HEAD), lambda h, b: (b, g * nh + h, 0, 0))
    ctx = lambda g: pl.BlockSpec((1, 1, lc, HEAD), lambda h, b: (g * nh + h, b, 0, 0))
    return pl.pallas_call(
        functools.partial(_na_kernel, rows=rows),
        out_shape=jax.ShapeDtypeStruct((bsz, nh, t, HEAD), BF16),
        grid=(nh, bsz),
        in_specs=[pl.BlockSpec(memory_space=pltpu.SMEM),
                  lat(0), lat(1), lat(2), lat(3), ctx(0), ctx(1)],
        out_specs=pl.BlockSpec((1, 1, t, HEAD), lambda h, b: (b, h, 0, 0)),
        scratch_shapes=[pltpu.VMEM((N_DROW, GRID_W, 2 * GRID_W), F32),
                        pltpu.VMEM((3, NA_QROWS * GRID_W, NA_KROWS * GRID_W), F32)],
        compiler_params=_params(("arbitrary", "arbitrary")),
        name="na",
    )(rpb.astype(F32).reshape(-1), proj, proj, proj, proj, proj_ctx, proj_ctx)


def _seg_cumsum(x, reverse):
    r_i = lax.broadcasted_iota(jnp.int32, (CHUNK, 2 * CHUNK), 0)
    c_i = lax.broadcasted_iota(jnp.int32, (CHUNK, 2 * CHUNK), 1) & (CHUNK - 1)
    tri2 = jnp.where((r_i <= c_i) if reverse else (r_i >= c_i), 1.0, 0.0).astype(BF16)
    hi = x.astype(BF16)
    lo = (x - hi.astype(F32)).astype(BF16)
    out = []
    for c in range(x.shape[0] // CHUNK):
        rows = slice(c * CHUNK, (c + 1) * CHUNK)
        out.append(jnp.dot(tri2, jnp.concatenate([hi[rows], lo[rows]], axis=0), preferred_element_type=F32))
    return jnp.concatenate(out, axis=0)


def _forget(f_pre, lb):
    sg = jax.nn.sigmoid(f_pre)
    one_m_lb = 1.0 - lb
    return one_m_lb * (1.0 - sg), jnp.log(lb + one_m_lb * sg)


def _lower_bound(logits):
    m = jnp.max(logits, axis=0, keepdims=True)
    e = jnp.exp(logits - m)
    return e[0:1] / jnp.sum(e, axis=0, keepdims=True)


def _gates(f_pre, lb, reverse):
    kk, logf = _forget(f_pre, lb)
    cum = _seg_cumsum(logf, reverse).reshape(-1, CHUNK, HEAD)
    mid = CHUNK // 2
    ref = cum[:, mid:mid + 1] if reverse else cum[:, mid - 1:mid]
    last = cum[:, 0:1] if reverse else cum[:, CHUNK - 1:CHUNK]
    return kk.reshape(cum.shape), cum, ref, last


def _ctx_state(f_pre, v, lb, reverse):
    kk, cum, _, last = _gates(f_pre, lb, reverse)
    kl = (kk * jnp.exp(last - cum)).astype(BF16)
    dec = jnp.exp(last)
    tn = (((0,), (0,)), ((), ()))
    n_chunks = kl.shape[0]
    st = jnp.zeros((HEAD, HEAD), F32)
    for c in (reversed(range(n_chunks)) if reverse else range(n_chunks)):
        u_t = lax.dot_general(v[c * CHUNK:(c + 1) * CHUNK], kl[c], tn, preferred_element_type=F32)
        st = st * dec[c] + u_t
    return st


def _hg_prepare(qs, f_pre, lb, reverse):
    kk, cum, ref, last = _gates(f_pre, lb, reverse)
    e = jnp.exp(cum - ref)
    qd = qs.reshape(cum.shape) * e
    kd = kk / e
    qe = qd * jnp.exp(ref)
    kl = kd * jnp.exp(last - ref)
    flat = lambda a: a.reshape(-1, HEAD).astype(BF16)
    return flat(qd), flat(kd), flat(qe), flat(kl), jnp.exp(last).reshape(-1, HEAD)


def _hg_kernel(q_ref, ff_ref, fb_ref, i_ref, g_ref, ffc_ref, fbc_ref, ic_ref, lbf_ref, lbb_ref, nw_ref,
               o_ref, qs_s, qd_s, kd_s, qe_s, kl_s, dec_s, ob_s, *, t):
    n_tiles = t // HG_TILE
    lb_f = _lower_bound(lbf_ref[...])
    lb_b = _lower_bound(lbb_ref[...])
    nw = nw_ref[...]
    r_i = lax.broadcasted_iota(jnp.int32, (CHUNK, CHUNK), 0)
    c_i = lax.broadcasted_iota(jnp.int32, (CHUNK, CHUNK), 1)
    nt = (((1,), (1,)), ((), ()))
    tn = (((0,), (0,)), ((), ()))

    def prepare(tix, f_ref, lb, reverse, first):
        rs = pl.ds(pl.multiple_of(tix * HG_TILE, HG_TILE), HG_TILE)
        if first:
            q = q_ref[0, 0, rs, :].astype(F32)
            qs = q * jax.nn.sigmoid(q)
        else:
            qs = qs_s[rs, :]
        return (qs if first else None,) + _hg_prepare(qs, f_ref[0, 0, rs, :].astype(F32), lb, reverse)

    def stage(tix, vals):
        rs = pl.ds(pl.multiple_of(tix * HG_TILE, HG_TILE), HG_TILE)
        qs, qd, kd, qe, kl, dec = vals
        if qs is not None:
            qs_s[rs, :] = qs
        qd_s[rs, :] = qd
        kd_s[rs, :] = kd
        qe_s[rs, :] = qe
        kl_s[rs, :] = kl
        dec_s[tix] = dec

    def scan(tix, st, reverse, emit):
        tri = (r_i <= c_i) if reverse else (r_i >= c_i)
        order = [HG_CHUNKS - 1 - u if reverse else u for u in range(HG_CHUNKS)]
        cs = {j: pl.ds(pl.multiple_of(tix * HG_TILE + j * CHUNK, CHUNK), CHUNK) for j in order}
        v = {j: i_ref[0, 0, cs[j], :] for j in order}
        a = {j: lax.dot_general(qd_s[cs[j], :], kd_s[cs[j], :], nt, preferred_element_type=F32) for j in order}
        u_t = {j: lax.dot_general(v[j], kl_s[cs[j], :], tn, preferred_element_type=F32) for j in order}
        dec = dec_s[tix]
        for j in order:
            o = (jnp.dot(jnp.where(tri, a[j], 0.0).astype(BF16), v[j], preferred_element_type=F32)
                 + lax.dot_general(qe_s[cs[j], :], st.astype(BF16), nt, preferred_element_type=F32))
            emit(cs[j], o)
            st = st * dec[j:j + 1] + u_t[j]
        return st

    def direction(f_ref, fc_ref, lb, reverse, first, emit):
        tile_of = (lambda n: n_tiles - 1 - n) if reverse else (lambda n: n)
        stage(tile_of(0), prepare(tile_of(0), f_ref, lb, reverse, first))
        st = _ctx_state(fc_ref[0, 0].astype(F32), ic_ref[0, 0], lb, reverse)

        def trip(n, st):
            nxt = tile_of(jnp.minimum(n + 1, n_tiles - 1))
            vals = prepare(nxt, f_ref, lb, reverse, first)
            st = scan(tile_of(n), st, reverse, emit)
            stage(nxt, vals)
            return st

        lax.fori_loop(0, n_tiles, trip, st)

    def emit_backward(cs, o):
        ob_s[cs, :] = o

    def emit_forward(cs, o):
        o = o + ob_s[cs, :]
        o = o * lax.rsqrt(jnp.mean(o * o, axis=-1, keepdims=True) + RMS_EPS) * nw
        g = g_ref[0, 0, cs, :].astype(F32)
        o_ref[0, 0, cs, :] = (o * (g * jax.nn.sigmoid(g))).astype(BF16)

    direction(fb_ref, fbc_ref, lb_b, True, True, emit_backward)
    direction(ff_ref, ffc_ref, lb_f, False, False, emit_forward)


def _hgrn2(proj, proj_ctx, lb_f, lb_b, norm_w, nh):
    bsz, _, t, _ = proj.shape
    lc = proj_ctx.shape[2]
    n_slots = lb_f.shape[0]
    assert t % HG_TILE == 0 and lc % CHUNK == 0
    lat = lambda g: pl.BlockSpec((1, 1, t, HEAD), lambda b, h: (b, g * nh + h, 0, 0))
    ctx = lambda g: pl.BlockSpec((1, 1, lc, HEAD), lambda b, h: (g * nh + h, b, 0, 0))
    lbs = pl.BlockSpec((n_slots, HEAD), lambda b, h: (0, h))
    staged = pltpu.VMEM((t, HEAD), BF16)
    return pl.pallas_call(
        functools.partial(_hg_kernel, t=t),
        out_shape=jax.ShapeDtypeStruct((bsz, nh, t, HEAD), BF16),
        grid=(bsz, nh),
        in_specs=[lat(4), lat(5), lat(6), lat(7), lat(8), ctx(2), ctx(3), ctx(4), lbs, lbs,
                  pl.BlockSpec((1, HEAD), lambda b, h: (0, 0))],
        out_specs=pl.BlockSpec((1, 1, t, HEAD), lambda b, h: (b, h, 0, 0)),
        scratch_shapes=[pltpu.VMEM((t, HEAD), F32), staged, staged, staged, staged,
                        pltpu.VMEM((t // HG_TILE, HG_CHUNKS, HEAD), F32), pltpu.VMEM((t, HEAD), F32)],
        compiler_params=_params(("arbitrary", "arbitrary")),
        name="hgrn2",
    )(proj, proj, proj, proj, proj, proj_ctx, proj_ctx, proj_ctx, lb_f, lb_b, norm_w)


def _merge_kernel(ya_ref, yb_ref, wa_ref, wb_ref, ga_ref, gb_ref, o_ref, ya_s, yb_s):
    @pl.when(pl.program_id(2) == 0)
    def _():
        for h in range(ya_ref.shape[1]):
            ya_s[:, h * HEAD:(h + 1) * HEAD] = ya_ref[0, h]
            yb_s[:, h * HEAD:(h + 1) * HEAD] = yb_ref[0, h]

    ta = jnp.dot(ya_s[...], wa_ref[...], preferred_element_type=F32)
    tb = jnp.dot(yb_s[...], wb_ref[...], preferred_element_type=F32)
    for s in range(ga_ref.shape[1]):
        cs = slice(s * HEAD, (s + 1) * HEAD)
        m = (jax.nn.sigmoid(ga_ref[0, s].astype(F32)) * ta[:, cs]
             + jax.nn.sigmoid(gb_ref[0, s].astype(F32)) * tb[:, cs])
        o_ref[0, :, cs] = m.astype(BF16)


def _merge(ya, yb, w_pa, w_pb, proj, nh):
    bsz, _, t, _ = ya.shape
    d = w_pa.shape[1]
    tm = _tile(t, 512)
    tn = _tile(nh * HEAD, 1024)
    sub = tn // HEAD
    ga0 = 9 * nh // sub
    gb0 = 11 * nh // sub
    assert (9 * nh) % sub == 0 and (11 * nh) % sub == 0
    yspec = pl.BlockSpec((1, nh, tm, HEAD), lambda b, i, j: (b, 0, i, 0))
    wspec = pl.BlockSpec((nh * HEAD, tn), lambda b, i, j: (0, j))
    return pl.pallas_call(
        _merge_kernel,
        out_shape=jax.ShapeDtypeStruct((bsz, t, d), BF16),
        grid=(bsz, t // tm, d // tn),
        in_specs=[yspec, yspec, wspec, wspec,
                  pl.BlockSpec((1, sub, tm, HEAD), lambda b, i, j: (b, ga0 + j, i, 0)),
                  pl.BlockSpec((1, sub, tm, HEAD), lambda b, i, j: (b, gb0 + j, i, 0))],
        out_specs=pl.BlockSpec((1, tm, tn), lambda b, i, j: (b, i, j)),
        scratch_shapes=[pltpu.VMEM((tm, nh * HEAD), BF16), pltpu.VMEM((tm, nh * HEAD), BF16)],
        compiler_params=_params(("arbitrary", "arbitrary", "arbitrary")),
        name="merge",
    )(ya, yb, w_pa, w_pb, proj, proj)


def _out_kernel(m_ref, w_ref, x_ref, gate_ref, lng_ref, lnb_ref, o_ref, *, alpha, tn):
    j = pl.program_id(2)
    out = jnp.dot(m_ref[0], w_ref[...], preferred_element_type=F32)
    col = pl.multiple_of(j * tn, tn)
    o_ref[0, :, pl.ds(col, tn)] = alpha * x_ref[0] + gate_ref[0] * out

    @pl.when(j == pl.num_programs(2) - 1)
    def _():
        def norm_rows(i, carry):
            rs = pl.ds(pl.multiple_of(i * LN_ROWS, LN_ROWS), LN_ROWS)
            r = o_ref[0, rs, :]
            mu = jnp.mean(r, axis=-1, keepdims=True)
            rc = r - mu
            var = jnp.mean(rc * rc, axis=-1, keepdims=True)
            o_ref[0, rs, :] = rc * lax.rsqrt(var + LN_EPS) * lng_ref[...] + lnb_ref[...]
            return carry

        lax.fori_loop(0, o_ref.shape[1] // LN_ROWS, norm_rows, 0)


def _out(m, w_out, x, gate, ln_g, ln_b, alpha):
    bsz, t, d = x.shape
    tm = _tile(t, 512)
    tn = _tile(d, 1024)
    return pl.pallas_call(
        functools.partial(_out_kernel, alpha=alpha, tn=tn),
        out_shape=jax.ShapeDtypeStruct((bsz, t, d), F32),
        grid=(bsz, t // tm, d // tn),
        in_specs=[pl.BlockSpec((1, tm, d), lambda b, i, j: (b, i, 0)),
                  pl.BlockSpec((d, tn), lambda b, i, j: (0, j)),
                  pl.BlockSpec((1, tm, tn), lambda b, i, j: (b, i, j)),
                  pl.BlockSpec((1, 1, tn), lambda b, i, j: (b, 0, j)),
                  pl.BlockSpec((1, d), lambda b, i, j: (0, 0)),
                  pl.BlockSpec((1, d), lambda b, i, j: (0, 0))],
        out_specs=pl.BlockSpec((1, tm, d), lambda b, i, j: (b, i, 0)),
        compiler_params=_params(("arbitrary", "arbitrary", "arbitrary")),
        name="out",
    )(m, w_out, x, gate, ln_g, ln_b)


def _layer(x, ctx, cond, w_ada, b_ada, w_in, rpb, lb_f, lb_b, norm_w, w_pa, w_pb, w_out, ln_g, ln_b, alpha):
    bsz, t, d = x.shape
    nh = (d // 2) // HEAD
    grp = d // 2
    tn = _tile(grp, 1024)
    per = grp // tn

    mod = _adaln(cond, w_ada, b_ada.reshape(1, -1))
    shift, scale, gate = (mod[:3, k * d:(k + 1) * d].reshape(3, 1, d) for k in range(3))
    ctx_row = bsz

    h_lat = _ln_mod(x, shift, scale, lambda b: b)
    h_ctx = _ln_mod(ctx, shift, scale, lambda b: ctx_row)

    first, w_in16 = _proj_first(h_lat, w_in)
    proj, w_pa16, w_pb16, w_out16 = _proj(h_lat, w_in16, 13 * per, lambda j: j, tn, "proj_lat", into=first,
                                          skip_blocks=1, cast=(w_pa, w_pb, w_out))
    proj_ctx = _proj(h_ctx.reshape(1, -1, d), w_in16, 5 * per,
                     lambda j: jnp.where(j < 2 * per, j + per, j + 3 * per), tn, "proj_ctx")
    proj_ctx = proj_ctx.reshape(5 * nh, bsz, ctx.shape[1], HEAD)

    y_a = _na(proj, proj_ctx, rpb, nh)
    y_b = _hgrn2(proj, proj_ctx, lb_f, lb_b, norm_w.reshape(1, HEAD), nh)
    m = _merge(y_a, y_b, w_pa16, w_pb16, proj, nh)
    return _out(m, w_out16, x, gate, ln_g.reshape(1, d), ln_b.reshape(1, d), alpha)


def kernel(x, c, ctx, c_ctx, w_ada, b_ada, w_in, na_rpb, hg_lb_fwd, hg_lb_bwd, hg_norm_w, w_pa, w_pb, w_out,
           ln_g, ln_b):
    depth = w_ada.shape[0]
    assert depth == 1, "the context stream update of deeper stacks is not implemented"
    bsz, _, d = x.shape
    alpha = (2.0 * depth) ** 0.25
    cond = jnp.concatenate([c, c_ctx[None], jnp.zeros((8 - bsz - 1, d), c.dtype)], axis=0)
    return _layer(x, ctx, cond, w_ada[0], b_ada[0], w_in[0], na_rpb[0], hg_lb_fwd, hg_lb_bwd, hg_norm_w[0],
                  w_pa[0], w_pb[0], w_out[0], ln_g[0], ln_b[0], alpha)
```

```python
import functools
import math

import jax
import jax.numpy as jnp
from jax import lax
from jax.experimental import pallas as pl
from jax.experimental.pallas import tpu as pltpu

F32 = jnp.float32
BF16 = jnp.bfloat16

GRID_W = 64
HEAD = 128
NA_KH = 8
NA_KW = 16
NA_QROWS = 4
NA_KROWS = 12
NA_UNROLL = 4
CHUNK = 64
HG_CHUNKS = 16
HG_TILE = HG_CHUNKS * CHUNK
LN_ROWS = 64
LOG2E = math.log2(math.e)
LN_EPS = 1e-6
RMS_EPS = 1e-6
VMEM_LIMIT = 56 * 1024 * 1024


def _params(semantics, vmem=VMEM_LIMIT):
    return pltpu.CompilerParams(dimension_semantics=semantics, vmem_limit_bytes=vmem)


def _tile(n, pref):
    t = min(n, pref)
    assert n % t == 0, (n, t)
    return t


def _adaln_kernel(c_ref, w_ref, b_ref, o_ref):
    c = c_ref[...]
    s = c * jax.nn.sigmoid(c)
    s_hi = s.astype(BF16)
    s_lo = (s - s_hi.astype(F32)).astype(BF16)
    w = w_ref[...]
    w_hi = w.astype(BF16)
    w_lo = (w - w_hi.astype(F32)).astype(BF16)
    n = s.shape[0]
    lhs = jnp.concatenate([s_hi, s_lo], axis=0)
    a = jnp.dot(lhs, w_hi, preferred_element_type=F32)
    b = jnp.dot(s_hi, w_lo, preferred_element_type=F32)
    o_ref[...] = a[:n] + a[n:] + b + b_ref[...]


def _adaln(cond, w, b):
    d, n = w.shape
    tn = _tile(n, 512)
    return pl.pallas_call(
        _adaln_kernel,
        out_shape=jax.ShapeDtypeStruct((cond.shape[0], n), F32),
        grid=(n // tn,),
        in_specs=[pl.BlockSpec(cond.shape, lambda j: (0, 0)),
                  pl.BlockSpec((d, tn), lambda j: (0, j)),
                  pl.BlockSpec((1, tn), lambda j: (0, j))],
        out_specs=pl.BlockSpec((cond.shape[0], tn), lambda j: (0, j)),
        compiler_params=_params(("arbitrary",)),
        name="adaln",
    )(cond, w, b)


def _ln_mod_kernel(x_ref, shift_ref, scale_ref, o_ref):
    x = x_ref[0]
    mu = jnp.mean(x, axis=-1, keepdims=True)
    xc = x - mu
    var = jnp.mean(xc * xc, axis=-1, keepdims=True)
    y = xc * lax.rsqrt(var + LN_EPS)
    o_ref[0] = (y * (1.0 + scale_ref[0]) + shift_ref[0]).astype(BF16)


def _ln_mod(x, shift, scale, cond_row):
    bsz, t, d = x.shape
    tm = _tile(t, 512)
    cmap = lambda b, i: (cond_row(b), 0, 0)
    return pl.pallas_call(
        _ln_mod_kernel,
        out_shape=jax.ShapeDtypeStruct((bsz, t, d), BF16),
        grid=(bsz, t // tm),
        in_specs=[pl.BlockSpec((1, tm, d), lambda b, i: (b, i, 0)),
                  pl.BlockSpec((1, 1, d), cmap),
                  pl.BlockSpec((1, 1, d), cmap)],
        out_specs=pl.BlockSpec((1, tm, d), lambda b, i: (b, i, 0)),
        compiler_params=_params(("arbitrary", "arbitrary")),
        name="ln_mod",
    )(x, shift, scale)


def _store_head_major(o_ref, acc):
    for s in range(o_ref.shape[1]):
        o_ref[0, s] = acc[:, s * HEAD:(s + 1) * HEAD].astype(BF16)


def _proj_kernel(h_ref, w_ref, *refs, n_cast, aliased):
    refs = refs[1:] if aliased else refs
    _store_head_major(refs[n_cast], jnp.dot(h_ref[0], w_ref[...], preferred_element_type=F32))
    for src, dst in zip(refs[:n_cast], refs[n_cast + 1:]):
        dst[...] = src[...].astype(BF16)


def _proj_cast_kernel(h_ref, w_ref, o_ref, w16_ref):
    w16 = w_ref[...].astype(BF16)
    w16_ref[...] = w16
    _store_head_major(o_ref, jnp.dot(h_ref[0], w16, preferred_element_type=F32))


def _proj_first(h, w):
    bsz, t, d = h.shape
    n = w.shape[1]
    tm = _tile(t, 1024)
    tn = next(c for c in (512, 256, HEAD) if n % c == 0)
    sub = tn // HEAD
    return pl.pallas_call(
        _proj_cast_kernel,
        out_shape=(jax.ShapeDtypeStruct((bsz, n // HEAD, t, HEAD), BF16), jax.ShapeDtypeStruct((d, n), BF16)),
        grid=(n // tn,),
        in_specs=[pl.BlockSpec((1, tm, d), lambda j: (0, 0, 0)),
                  pl.BlockSpec((d, tn), lambda j: (0, j))],
        out_specs=[pl.BlockSpec((1, sub, tm, HEAD), lambda j: (0, j, 0, 0)),
                   pl.BlockSpec((d, tn), lambda j: (0, j))],
        compiler_params=_params(("arbitrary",)),
        name="proj_first",
    )(h, w)


def _proj(h, w, n_tiles, col_tile, tn, name, into=None, skip_blocks=0, cast=()):
    bsz, t, d = h.shape
    tm = _tile(t, 1024)
    nb = t // tm
    sub = tn // HEAD
    n_rows = bsz * nb - skip_blocks
    row = lambda r: ((r + skip_blocks) // nb, (r + skip_blocks) % nb)
    in_specs = [pl.BlockSpec((1, tm, d), lambda r, j: (*row(r), 0)),
                pl.BlockSpec((d, tn), lambda r, j: (0, col_tile(j)))]
    args = [h, w]
    if into is not None:
        in_specs.append(pl.BlockSpec(memory_space=pl.ANY))
        args.append(into)
    out_specs = [pl.BlockSpec((1, sub, tm, HEAD), lambda r, j: (row(r)[0], j, row(r)[1], 0))]
    out_shape = [jax.ShapeDtypeStruct((bsz, n_tiles * sub, t, HEAD), BF16)]
    for cw in cast:
        rows, cols = cw.shape
        split = next(s for s in range(n_rows * n_tiles, 0, -1) if rows % s == 0 and (rows // s) % 16 == 0)
        spec = pl.BlockSpec((rows // split, cols),
                            lambda r, j, split=split: (jnp.minimum(r * n_tiles + j, split - 1), 0))
        in_specs.append(spec)
        out_specs.append(spec)
        out_shape.append(jax.ShapeDtypeStruct(cw.shape, BF16))
    res = pl.pallas_call(
        functools.partial(_proj_kernel, n_cast=len(cast), aliased=into is not None),
        out_shape=out_shape,
        grid=(n_rows, n_tiles),
        in_specs=in_specs,
        out_specs=out_specs,
        input_output_aliases={} if into is None else {2: 0},
        compiler_params=_params(("arbitrary", "arbitrary")),
        name=name,
    )(*args, *cast)
    return res if cast else res[0]


N_DROW = 2 * NA_KH - 1
N_DCOL = 2 * NA_KW - 1


def _na_block_kinds(rows):
    return ((0, 0), (NA_QROWS, 0), (rows - NA_QROWS, rows - NA_KROWS))


def _build_bias(rpb_ref, head, band_s, bias_s, rows):
    shape = (GRID_W, 2 * GRID_W)
    qc = lax.broadcasted_iota(jnp.int32, shape, 0)
    lane = lax.broadcasted_iota(jnp.int32, shape, 1)
    kc = lane & (GRID_W - 1)
    d_col = jnp.clip(kc - qc, -(NA_KW - 1), NA_KW - 1) + (NA_KW - 1)
    col_start = jnp.clip(qc - NA_KW // 2, 0, GRID_W - NA_KW)
    col_off = kc - col_start
    neg = jnp.full(shape, -jnp.inf, F32)
    base = head * (N_DROW * N_DCOL)
    for d in range(N_DROW):
        acc = neg
        for j in range(N_DCOL):
            acc = jnp.where(d_col == j, rpb_ref[base + d * N_DCOL + j] * LOG2E, acc)
        band_s[d] = jnp.where(col_off < 0, neg, jnp.where(col_off < NA_KW, acc, neg))
    left = lane < GRID_W
    for kind, (r0, kb) in enumerate(_na_block_kinds(rows)):
        for qi in range(NA_QROWS):
            r = r0 + qi
            row_start = min(max(r - NA_KH // 2, 0), rows - NA_KH)

            def band(kr):
                return band_s[kr - r + NA_KH - 1] if row_start <= kr < row_start + NA_KH else neg

            for jp in range(NA_KROWS // 2):
                kr = kb + 2 * jp
                bias_s[kind, qi * GRID_W:(qi + 1) * GRID_W, jp * 2 * GRID_W:(jp + 1) * 2 * GRID_W] = (
                    jnp.where(left, band(kr), band(kr + 1)))


def _na_kernel(rpb_ref, q_ref, k_ref, v_ref, z_ref, kc_ref, vc_ref, o_ref, band_s, bias_s, *, rows):
    nq = NA_QROWS * GRID_W
    nk = NA_KROWS * GRID_W
    n_blocks = rows // NA_QROWS
    scale = HEAD ** -0.5 * LOG2E
    kc = kc_ref[0, 0]
    vc = vc_ref[0, 0]
    nt = (((1,), (1,)), ((), ()))

    @pl.when(pl.program_id(1) == 0)
    def _():
        _build_bias(rpb_ref, pl.program_id(0), band_s, bias_s, rows)

    def scores(i):
        r0 = i * NA_QROWS
        kb = jnp.clip(r0 - NA_KH // 2, 0, rows - NA_KROWS)
        kind = jnp.where(i == 0, 0, jnp.where(i == n_blocks - 1, 2, 1))
        q_off = pl.multiple_of(r0 * GRID_W, GRID_W)
        k_off = pl.multiple_of(kb * GRID_W, GRID_W)
        q = (q_ref[0, 0, pl.ds(q_off, nq), :].astype(F32) * scale).astype(BF16)
        s_loc = lax.dot_general(q, k_ref[0, 0, pl.ds(k_off, nk), :], nt, preferred_element_type=F32)
        s_ctx = lax.dot_general(q, kc, nt, preferred_element_type=F32)
        return q_off, k_off, s_loc + bias_s[kind], s_ctx

    def attend(q_off, k_off, s_loc, s_ctx):
        m = jnp.maximum(jnp.max(s_loc, axis=-1, keepdims=True), jnp.max(s_ctx, axis=-1, keepdims=True))
        p_loc = jnp.exp2(s_loc - m)
        p_ctx = jnp.exp2(s_ctx - m)
        denom = jnp.sum(p_loc, axis=-1, keepdims=True) + jnp.sum(p_ctx, axis=-1, keepdims=True)
        o = (jnp.dot(p_loc.astype(BF16), v_ref[0, 0, pl.ds(k_off, nk), :], preferred_element_type=F32)
             + jnp.dot(p_ctx.astype(BF16), vc, preferred_element_type=F32))
        o = o / denom
        z = z_ref[0, 0, pl.ds(q_off, nq), :].astype(F32)
        o_ref[0, 0, pl.ds(q_off, nq), :] = (o * (z * jax.nn.sigmoid(z))).astype(BF16)

    def blocks(n, carry):
        staged = [scores(n * NA_UNROLL + u) for u in range(NA_UNROLL)]
        for args in staged:
            attend(*args)
        return carry

    lax.fori_loop(0, n_blocks // NA_UNROLL, blocks, 0)


def _na(proj, proj_ctx, rpb, nh):
    bsz, _, t, _ = proj.shape
    lc = proj_ctx.shape[2]
    rows = t // GRID_W
    assert rows % (NA_QROWS * NA_UNROLL) == 0 and rows >= NA_KROWS + NA_QROWS
    assert rpb.shape == (nh, N_DROW, N_DCOL)
    lat = lambda g: pl.BlockSpec((1, 1, t, HEAD), lambda h, b: (b, g * nh + h, 0, 0))
    ctx = lambda g: pl.BlockSpec((1, 1, lc, HEAD), lambda h, b: (g * nh + h, b, 0, 0))
    return pl.pallas_call(
        functools.partial(_na_kernel, rows=rows),
        out_shape=jax.ShapeDtypeStruct((bsz, nh, t, HEAD), BF16),
        grid=(nh, bsz),
        in_specs=[pl.BlockSpec(memory_space=pltpu.SMEM),
                  lat(0), lat(1), lat(2), lat(3), ctx(0), ctx(1)],
        out_specs=pl.BlockSpec((1, 1, t, HEAD), lambda h, b: (b, h, 0, 0)),
        scratch_shapes=[pltpu.VMEM((N_DROW, GRID_W, 2 * GRID_W), F32),
                        pltpu.VMEM((3, NA_QROWS * GRID_W, NA_KROWS * GRID_W), F32)],
        compiler_params=_params(("arbitrary", "arbitrary")),
        name="na",
    )(rpb.astype(F32).reshape(-1), proj, proj, proj, proj, proj_ctx, proj_ctx)


def _seg_cumsum(x, reverse):
    r_i = lax.broadcasted_iota(jnp.int32, (CHUNK, 2 * CHUNK), 0)
    c_i = lax.broadcasted_iota(jnp.int32, (CHUNK, 2 * CHUNK), 1) & (CHUNK - 1)
    tri2 = jnp.where((r_i <= c_i) if reverse else (r_i >= c_i), 1.0, 0.0).astype(BF16)
    hi = x.astype(BF16)
    lo = (x - hi.astype(F32)).astype(BF16)
    out = []
    for c in range(x.shape[0] // CHUNK):
        rows = slice(c * CHUNK, (c + 1) * CHUNK)
        out.append(jnp.dot(tri2, jnp.concatenate([hi[rows], lo[rows]], axis=0), preferred_element_type=F32))
    return jnp.concatenate(out, axis=0)


def _forget(f_pre, lb):
    sg = jax.nn.sigmoid(f_pre)
    one_m_lb = 1.0 - lb
    return one_m_lb * (1.0 - sg), jnp.log(lb + one_m_lb * sg)


def _lower_bound(logits):
    m = jnp.max(logits, axis=0, keepdims=True)
    e = jnp.exp(logits - m)
    return e[0:1] / jnp.sum(e, axis=0, keepdims=True)


def _gates(f_pre, lb, reverse):
    kk, logf = _forget(f_pre, lb)
    cum = _seg_cumsum(logf, reverse).reshape(-1, CHUNK, HEAD)
    mid = CHUNK // 2
    ref = cum[:, mid:mid + 1] if reverse else cum[:, mid - 1:mid]
    last = cum[:, 0:1] if reverse else cum[:, CHUNK - 1:CHUNK]
    return kk.reshape(cum.shape), cum, ref, last


def _decay_columns(dec):
    pad = jnp.zeros((HEAD - dec.shape[0], HEAD), F32)
    return jnp.concatenate([dec, pad], axis=0).T


def _ctx_state(f_pre, v, lb, reverse):
    kk, cum, _, last = _gates(f_pre, lb, reverse)
    kl = (kk * jnp.exp(last - cum)).astype(BF16)
    dec_t = _decay_columns(jnp.exp(last).reshape(-1, HEAD))
    tn = (((0,), (0,)), ((), ()))
    n_chunks = kl.shape[0]
    st = jnp.zeros((HEAD, HEAD), F32)
    for c in (reversed(range(n_chunks)) if reverse else range(n_chunks)):
        u = lax.dot_general(kl[c], v[c * CHUNK:(c + 1) * CHUNK], tn, preferred_element_type=F32)
        st = st * dec_t[:, c:c + 1] + u
    return st


def _hg_prepare(qs, f_pre, lb, reverse):
    kk, cum, ref, last = _gates(f_pre, lb, reverse)
    e = jnp.exp(cum - ref)
    qd = qs.reshape(cum.shape) * e
    kd = kk / e
    qe = qd * jnp.exp(ref)
    kl = kd * jnp.exp(last - ref)
    flat = lambda a: a.reshape(-1, HEAD).astype(BF16)
    return flat(qd), flat(kd), flat(qe), flat(kl), jnp.exp(last).reshape(-1, HEAD)


def _hg_kernel(q_ref, ff_ref, fb_ref, i_ref, g_ref, ffc_ref, fbc_ref, ic_ref, lbf_ref, lbb_ref, nw_ref,
               o_ref, qs_s, qd_s, kd_s, qe_s, kl_s, dec_s, ob_s, *, t):
    n_tiles = t // HG_TILE
    lb_f = _lower_bound(lbf_ref[...])
    lb_b = _lower_bound(lbb_ref[...])
    nw = nw_ref[...]
    r_i = lax.broadcasted_iota(jnp.int32, (CHUNK, CHUNK), 0)
    c_i = lax.broadcasted_iota(jnp.int32, (CHUNK, CHUNK), 1)
    nt = (((1,), (1,)), ((), ()))
    tn = (((0,), (0,)), ((), ()))

    def prepare(tix, f_ref, lb, reverse, first):
        rs = pl.ds(pl.multiple_of(tix * HG_TILE, HG_TILE), HG_TILE)
        if first:
            q = q_ref[0, 0, rs, :].astype(F32)
            qs = q * jax.nn.sigmoid(q)
        else:
            qs = qs_s[rs, :]
        return (qs if first else None,) + _hg_prepare(qs, f_ref[0, 0, rs, :].astype(F32), lb, reverse)

    def stage(tix, vals):
        rs = pl.ds(pl.multiple_of(tix * HG_TILE, HG_TILE), HG_TILE)
        qs, qd, kd, qe, kl, dec = vals
        if qs is not None:
            qs_s[rs, :] = qs
        qd_s[rs, :] = qd
        kd_s[rs, :] = kd
        qe_s[rs, :] = qe
        kl_s[rs, :] = kl
        dec_s[tix] = dec

    def scan(tix, st, reverse, emit):
        tri = (r_i <= c_i) if reverse else (r_i >= c_i)
        order = [HG_CHUNKS - 1 - u if reverse else u for u in range(HG_CHUNKS)]
        cs = {j: pl.ds(pl.multiple_of(tix * HG_TILE + j * CHUNK, CHUNK), CHUNK) for j in order}
        v = {j: i_ref[0, 0, cs[j], :] for j in order}
        a = {j: lax.dot_general(qd_s[cs[j], :], kd_s[cs[j], :], nt, preferred_element_type=F32) for j in order}
        u = {j: lax.dot_general(kl_s[cs[j], :], v[j], tn, preferred_element_type=F32) for j in order}
        dec_t = _decay_columns(dec_s[tix])
        dec_b = {j: jnp.broadcast_to(dec_t[:, j:j + 1], (HEAD, HEAD)) for j in order}
        for j in order:
            lhs = jnp.concatenate([qe_s[cs[j], :], jnp.where(tri, a[j], 0.0).astype(BF16)], axis=1)
            rhs = jnp.concatenate([st.astype(BF16), v[j]], axis=0)
            emit(cs[j], jnp.dot(lhs, rhs, preferred_element_type=F32))
            st = st * dec_b[j] + u[j]
        return st

    def direction(f_ref, fc_ref, lb, reverse, first, emit):
        tile_of = (lambda n: n_tiles - 1 - n) if reverse else (lambda n: n)
        stage(tile_of(0), prepare(tile_of(0), f_ref, lb, reverse, first))
        st = _ctx_state(fc_ref[0, 0].astype(F32), ic_ref[0, 0], lb, reverse)

        def trip(n, st):
            nxt = tile_of(jnp.minimum(n + 1, n_tiles - 1))
            vals = prepare(nxt, f_ref, lb, reverse, first)
            st = scan(tile_of(n), st, reverse, emit)
            stage(nxt, vals)
            return st

        lax.fori_loop(0, n_tiles, trip, st)

    def emit_backward(cs, o):
        ob_s[cs, :] = o

    def emit_forward(cs, o):
        o = o + ob_s[cs, :]
        o = o * lax.rsqrt(jnp.mean(o * o, axis=-1, keepdims=True) + RMS_EPS) * nw
        g = g_ref[0, 0, cs, :].astype(F32)
        o_ref[0, 0, cs, :] = (o * (g * jax.nn.sigmoid(g))).astype(BF16)

    direction(fb_ref, fbc_ref, lb_b, True, True, emit_backward)
    direction(ff_ref, ffc_ref, lb_f, False, False, emit_forward)


def _hgrn2(proj, proj_ctx, lb_f, lb_b, norm_w, nh):
    bsz, _, t, _ = proj.shape
    lc = proj_ctx.shape[2]
    n_slots = lb_f.shape[0]
    assert t % HG_TILE == 0 and lc % CHUNK == 0
    lat = lambda g: pl.BlockSpec((1, 1, t, HEAD), lambda b, h: (b, g * nh + h, 0, 0))
    ctx = lambda g: pl.BlockSpec((1, 1, lc, HEAD), lambda b, h: (g * nh + h, b, 0, 0))
    lbs = pl.BlockSpec((n_slots, HEAD), lambda b, h: (0, h))
    staged = pltpu.VMEM((t, HEAD), BF16)
    return pl.pallas_call(
        functools.partial(_hg_kernel, t=t),
        out_shape=jax.ShapeDtypeStruct((bsz, nh, t, HEAD), BF16),
        grid=(bsz, nh),
        in_specs=[lat(4), lat(5), lat(6), lat(7), lat(8), ctx(2), ctx(3), ctx(4), lbs, lbs,
                  pl.BlockSpec((1, HEAD), lambda b, h: (0, 0))],
        out_specs=pl.BlockSpec((1, 1, t, HEAD), lambda b, h: (b, h, 0, 0)),
        scratch_shapes=[pltpu.VMEM((t, HEAD), F32), staged, staged, staged, staged,
                        pltpu.VMEM((t // HG_TILE, HG_CHUNKS, HEAD), F32), pltpu.VMEM((t, HEAD), F32)],
        compiler_params=_params(("arbitrary", "arbitrary")),
        name="hgrn2",
    )(proj, proj, proj, proj, proj, proj_ctx, proj_ctx, proj_ctx, lb_f, lb_b, norm_w)


def _merge_kernel(ya_ref, yb_ref, wa_ref, wb_ref, ga_ref, gb_ref, o_ref, ya_s, yb_s):
    @pl.when(pl.program_id(2) == 0)
    def _():
        for h in range(ya_ref.shape[1]):
            ya_s[:, h * HEAD:(h + 1) * HEAD] = ya_ref[0, h]
            yb_s[:, h * HEAD:(h + 1) * HEAD] = yb_ref[0, h]

    ta = jnp.dot(ya_s[...], wa_ref[...], preferred_element_type=F32)
    tb = jnp.dot(yb_s[...], wb_ref[...], preferred_element_type=F32)
    for s in range(ga_ref.shape[1]):
        cs = slice(s * HEAD, (s + 1) * HEAD)
        m = (jax.nn.sigmoid(ga_ref[0, s].astype(F32)) * ta[:, cs]
             + jax.nn.sigmoid(gb_ref[0, s].astype(F32)) * tb[:, cs])
        o_ref[0, :, cs] = m.astype(BF16)


def _merge(ya, yb, w_pa, w_pb, proj, nh):
    bsz, _, t, _ = ya.shape
    d = w_pa.shape[1]
    tm = _tile(t, 512)
    tn = _tile(nh * HEAD, 1024)
    sub = tn // HEAD
    ga0 = 9 * nh // sub
    gb0 = 11 * nh // sub
    assert (9 * nh) % sub == 0 and (11 * nh) % sub == 0
    yspec = pl.BlockSpec((1, nh, tm, HEAD), lambda b, i, j: (b, 0, i, 0))
    wspec = pl.BlockSpec((nh * HEAD, tn), lambda b, i, j: (0, j))
    return pl.pallas_call(
        _merge_kernel,
        out_shape=jax.ShapeDtypeStruct((bsz, t, d), BF16),
        grid=(bsz, t // tm, d // tn),
        in_specs=[yspec, yspec, wspec, wspec,
                  pl.BlockSpec((1, sub, tm, HEAD), lambda b, i, j: (b, ga0 + j, i, 0)),
                  pl.BlockSpec((1, sub, tm, HEAD), lambda b, i, j: (b, gb0 + j, i, 0))],
        out_specs=pl.BlockSpec((1, tm, tn), lambda b, i, j: (b, i, j)),
        scratch_shapes=[pltpu.VMEM((tm, nh * HEAD), BF16), pltpu.VMEM((tm, nh * HEAD), BF16)],
        compiler_params=_params(("arbitrary", "arbitrary", "arbitrary")),
        name="merge",
    )(ya, yb, w_pa, w_pb, proj, proj)


def _out_kernel(m_ref, w_ref, x_ref, gate_ref, lng_ref, lnb_ref, o_ref, *, alpha, tn):
    j = pl.program_id(2)
    out = jnp.dot(m_ref[0], w_ref[...], preferred_element_type=F32)
    col = pl.multiple_of(j * tn, tn)
    o_ref[0, :, pl.ds(col, tn)] = alpha * x_ref[0] + gate_ref[0] * out

    @pl.when(j == pl.num_programs(2) - 1)
    def _():
        def norm_rows(i, carry):
            rs = pl.ds(pl.multiple_of(i * LN_ROWS, LN_ROWS), LN_ROWS)
            r = o_ref[0, rs, :]
            mu = jnp.mean(r, axis=-1, keepdims=True)
            rc = r - mu
            var = jnp.mean(rc * rc, axis=-1, keepdims=True)
            o_ref[0, rs, :] = rc * lax.rsqrt(var + LN_EPS) * lng_ref[...] + lnb_ref[...]
            return carry

        lax.fori_loop(0, o_ref.shape[1] // LN_ROWS, norm_rows, 0)


def _out(m, w_out, x, gate, ln_g, ln_b, alpha):
    bsz, t, d = x.shape
    tm = _tile(t, 512)
    tn = _tile(d, 1024)
    return pl.pallas_call(
        functools.partial(_out_kernel, alpha=alpha, tn=tn),
        out_shape=jax.ShapeDtypeStruct((bsz, t, d), F32),
        grid=(bsz, t // tm, d // tn),
        in_specs=[pl.BlockSpec((1, tm, d), lambda b, i, j: (b, i, 0)),
                  pl.BlockSpec((d, tn), lambda b, i, j: (0, j)),
                  pl.BlockSpec((1, tm, tn), lambda b, i, j: (b, i, j)),
                  pl.BlockSpec((1, 1, tn), lambda b, i, j: (b, 0, j)),
                  pl.BlockSpec((1, d), lambda b, i, j: (0, 0)),
                  pl.BlockSpec((1, d), lambda b, i, j: (0, 0))],
        out_specs=pl.BlockSpec((1, tm, d), lambda b, i, j: (b, i, 0)),
        compiler_params=_params(("arbitrary", "arbitrary", "arbitrary")),
        name="out",
    )(m, w_out, x, gate, ln_g, ln_b)


def _layer(x, ctx, cond, w_ada, b_ada, w_in, rpb, lb_f, lb_b, norm_w, w_pa, w_pb, w_out, ln_g, ln_b, alpha):
    bsz, t, d = x.shape
    nh = (d // 2) // HEAD
    grp = d // 2
    tn = _tile(grp, 1024)
    per = grp // tn

    mod = _adaln(cond, w_ada, b_ada.reshape(1, -1))
    shift, scale, gate = (mod[:3, k * d:(k + 1) * d].reshape(3, 1, d) for k in range(3))
    ctx_row = bsz

    h_lat = _ln_mod(x, shift, scale, lambda b: b)
    h_ctx = _ln_mod(ctx, shift, scale, lambda b: ctx_row)

    first, w_in16 = _proj_first(h_lat, w_in)
    proj, w_pa16, w_pb16, w_out16 = _proj(h_lat, w_in16, 13 * per, lambda j: j, tn, "proj_lat", into=first,
                                          skip_blocks=1, cast=(w_pa, w_pb, w_out))
    proj_ctx = _proj(h_ctx.reshape(1, -1, d), w_in16, 5 * per,
                     lambda j: jnp.where(j < 2 * per, j + per, j + 3 * per), tn, "proj_ctx")
    proj_ctx = proj_ctx.reshape(5 * nh, bsz, ctx.shape[1], HEAD)

    y_a = _na(proj, proj_ctx, rpb, nh)
    y_b = _hgrn2(proj, proj_ctx, lb_f, lb_b, norm_w.reshape(1, HEAD), nh)
    m = _merge(y_a, y_b, w_pa16, w_pb16, proj, nh)
    return _out(m, w_out16, x, gate, ln_g.reshape(1, d), ln_b.reshape(1, d), alpha)


def kernel(x, c, ctx, c_ctx, w_ada, b_ada, w_in, na_rpb, hg_lb_fwd, hg_lb_bwd, hg_norm_w, w_pa, w_pb, w_out,
           ln_g, ln_b):
    depth = w_ada.shape[0]
    assert depth == 1, "the context stream update of deeper stacks is not implemented"
    bsz, _, d = x.shape
    alpha = (2.0 * depth) ** 0.25
    cond = jnp.concatenate([c, c_ctx[None], jnp.zeros((8 - bsz - 1, d), c.dtype)], axis=0)
    return _layer(x, ctx, cond, w_ada[0], b_ada[0], w_in[0], na_rpb[0], hg_lb_fwd, hg_lb_bwd, hg_norm_w[0],
                  w_pa[0], w_pb[0], w_out[0], ln_g[0], ln_b[0], alpha)
```

```python
import functools
import math

import jax
import jax.numpy as jnp
from jax import lax
from jax.experimental import pallas as pl
from jax.experimental.pallas import tpu as pltpu

F32 = jnp.float32
BF16 = jnp.bfloat16

GRID_W = 64
HEAD = 128
NA_KH = 8
NA_KW = 16
NA_QROWS = 4
NA_KROWS = 12
NA_UNROLL = 4
CHUNK = 64
HG_CHUNKS = 16
HG_TILE = HG_CHUNKS * CHUNK
LN_ROWS = 64
LOG2E = math.log2(math.e)
LN_EPS = 1e-6
RMS_EPS = 1e-6
VMEM_LIMIT = 56 * 1024 * 1024


def _params(semantics, vmem=VMEM_LIMIT):
    return pltpu.CompilerParams(dimension_semantics=semantics, vmem_limit_bytes=vmem)


def _tile(n, pref):
    t = min(n, pref)
    assert n % t == 0, (n, t)
    return t


def _adaln_kernel(c_ref, w_ref, b_ref, o_ref):
    c = c_ref[...]
    s = c * jax.nn.sigmoid(c)
    s_hi = s.astype(BF16)
    s_lo = (s - s_hi.astype(F32)).astype(BF16)
    w = w_ref[...]
    w_hi = w.astype(BF16)
    w_lo = (w - w_hi.astype(F32)).astype(BF16)
    n = s.shape[0]
    lhs = jnp.concatenate([s_hi, s_lo], axis=0)
    a = jnp.dot(lhs, w_hi, preferred_element_type=F32)
    b = jnp.dot(s_hi, w_lo, preferred_element_type=F32)
    o_ref[...] = a[:n] + a[n:] + b + b_ref[...]


def _adaln(cond, w, b):
    d, n = w.shape
    tn = _tile(n, 512)
    return pl.pallas_call(
        _adaln_kernel,
        out_shape=jax.ShapeDtypeStruct((cond.shape[0], n), F32),
        grid=(n // tn,),
        in_specs=[pl.BlockSpec(cond.shape, lambda j: (0, 0)),
                  pl.BlockSpec((d, tn), lambda j: (0, j)),
                  pl.BlockSpec((1, tn), lambda j: (0, j))],
        out_specs=pl.BlockSpec((cond.shape[0], tn), lambda j: (0, j)),
        compiler_params=_params(("arbitrary",)),
        name="adaln",
    )(cond, w, b)


def _ln_mod_kernel(x_ref, shift_ref, scale_ref, o_ref):
    x = x_ref[0]
    mu = jnp.mean(x, axis=-1, keepdims=True)
    xc = x - mu
    var = jnp.mean(xc * xc, axis=-1, keepdims=True)
    y = xc * lax.rsqrt(var + LN_EPS)
    o_ref[0] = (y * (1.0 + scale_ref[0]) + shift_ref[0]).astype(BF16)


def _ln_mod(x, shift, scale, cond_row):
    bsz, t, d = x.shape
    tm = _tile(t, 512)
    cmap = lambda b, i: (cond_row(b), 0, 0)
    return pl.pallas_call(
        _ln_mod_kernel,
        out_shape=jax.ShapeDtypeStruct((bsz, t, d), BF16),
        grid=(bsz, t // tm),
        in_specs=[pl.BlockSpec((1, tm, d), lambda b, i: (b, i, 0)),
                  pl.BlockSpec((1, 1, d), cmap),
                  pl.BlockSpec((1, 1, d), cmap)],
        out_specs=pl.BlockSpec((1, tm, d), lambda b, i: (b, i, 0)),
        compiler_params=_params(("arbitrary", "arbitrary")),
        name="ln_mod",
    )(x, shift, scale)


def _store_head_major(o_ref, acc):
    for s in range(o_ref.shape[1]):
        o_ref[0, s] = acc[:, s * HEAD:(s + 1) * HEAD].astype(BF16)


def _proj_kernel(h_ref, w_ref, *refs, n_cast, has_first):
    if has_first:
        first_ref, refs = refs[0], refs[1:]
    o_ref = refs[n_cast]

    def compute():
        _store_head_major(o_ref, jnp.dot(h_ref[0], w_ref[...], preferred_element_type=F32))

    if has_first:
        @pl.when(pl.program_id(0) == 0)
        def _():
            o_ref[...] = first_ref[...]

        pl.when(pl.program_id(0) > 0)(compute)
    else:
        compute()
    for src, dst in zip(refs[:n_cast], refs[n_cast + 1:]):
        dst[...] = src[...].astype(BF16)


def _proj_cast_kernel(h_ref, w_ref, o_ref, w16_ref):
    w16 = w_ref[...].astype(BF16)
    w16_ref[...] = w16
    _store_head_major(o_ref, jnp.dot(h_ref[0], w16, preferred_element_type=F32))


def _proj_first(h, w):
    _, t, d = h.shape
    n = w.shape[1]
    tm = _tile(t, 1024)
    tn = next(c for c in (512, 256, HEAD) if n % c == 0)
    sub = tn // HEAD
    return pl.pallas_call(
        _proj_cast_kernel,
        out_shape=(jax.ShapeDtypeStruct((1, n // HEAD, tm, HEAD), BF16), jax.ShapeDtypeStruct((d, n), BF16)),
        grid=(n // tn,),
        in_specs=[pl.BlockSpec((1, tm, d), lambda j: (0, 0, 0)),
                  pl.BlockSpec((d, tn), lambda j: (0, j))],
        out_specs=[pl.BlockSpec((1, sub, tm, HEAD), lambda j: (0, j, 0, 0)),
                   pl.BlockSpec((d, tn), lambda j: (0, j))],
        compiler_params=_params(("arbitrary",)),
        name="proj_first",
    )(h, w)


def _proj(h, w, n_tiles, col_tile, tn, name, first=None, cast=()):
    bsz, t, d = h.shape
    tm = _tile(t, 1024)
    nb = t // tm
    sub = tn // HEAD
    n_rows = bsz * nb
    has_first = first is not None
    mm_row = (lambda r: jnp.maximum(r, 1)) if has_first else (lambda r: r)
    mm_col = (lambda r, j: jnp.where(r == 0, 0, j)) if has_first else (lambda r, j: j)
    in_specs = [pl.BlockSpec((1, tm, d), lambda r, j: (mm_row(r) // nb, mm_row(r) % nb, 0)),
                pl.BlockSpec((d, tn), lambda r, j: (0, col_tile(mm_col(r, j))))]
    args = [h, w]
    if has_first:
        assert first.shape == (1, n_tiles * sub, tm, HEAD)
        in_specs.append(pl.BlockSpec((1, sub, tm, HEAD), lambda r, j: (0, jnp.where(r == 0, j, n_tiles - 1), 0, 0)))
        args.append(first)
    out_specs = [pl.BlockSpec((1, sub, tm, HEAD), lambda r, j: (r // nb, j, r % nb, 0))]
    out_shape = [jax.ShapeDtypeStruct((bsz, n_tiles * sub, t, HEAD), BF16)]
    for cw in cast:
        rows, cols = cw.shape
        split = next(s for s in range(n_rows * n_tiles, 0, -1) if rows % s == 0 and (rows // s) % 16 == 0)
        spec = pl.BlockSpec((rows // split, cols),
                            lambda r, j, split=split: (jnp.minimum(r * n_tiles + j, split - 1), 0))
        in_specs.append(spec)
        out_specs.append(spec)
        out_shape.append(jax.ShapeDtypeStruct(cw.shape, BF16))
    res = pl.pallas_call(
        functools.partial(_proj_kernel, n_cast=len(cast), has_first=has_first),
        out_shape=out_shape,
        grid=(n_rows, n_tiles),
        in_specs=in_specs,
        out_specs=out_specs,
        compiler_params=_params(("arbitrary", "arbitrary")),
        name=name,
    )(*args, *cast)
    return res if cast else res[0]


N_DROW = 2 * NA_KH - 1
N_DCOL = 2 * NA_KW - 1


def _na_block_kinds(rows):
    return ((0, 0), (NA_QROWS, 0), (rows - NA_QROWS, rows - NA_KROWS))


def _build_bias(rpb_ref, head, band_s, bias_s, rows):
    shape = (GRID_W, 2 * GRID_W)
    qc = lax.broadcasted_iota(jnp.int32, shape, 0)
    lane = lax.broadcasted_iota(jnp.int32, shape, 1)
    kc = lane & (GRID_W - 1)
    d_col = jnp.clip(kc - qc, -(NA_KW - 1), NA_KW - 1) + (NA_KW - 1)
    col_start = jnp.clip(qc - NA_KW // 2, 0, GRID_W - NA_KW)
    col_off = kc - col_start
    neg = jnp.full(shape, -jnp.inf, F32)
    base = head * (N_DROW * N_DCOL)
    for d in range(N_DROW):
        acc = neg
        for j in range(N_DCOL):
            acc = jnp.where(d_col == j, rpb_ref[base + d * N_DCOL + j] * LOG2E, acc)
        band_s[d] = jnp.where(col_off < 0, neg, jnp.where(col_off < NA_KW, acc, neg))
    left = lane < GRID_W
    for kind, (r0, kb) in enumerate(_na_block_kinds(rows)):
        for qi in range(NA_QROWS):
            r = r0 + qi
            row_start = min(max(r - NA_KH // 2, 0), rows - NA_KH)

            def band(kr):
                return band_s[kr - r + NA_KH - 1] if row_start <= kr < row_start + NA_KH else neg

            for jp in range(NA_KROWS // 2):
                kr = kb + 2 * jp
                bias_s[kind, qi * GRID_W:(qi + 1) * GRID_W, jp * 2 * GRID_W:(jp + 1) * 2 * GRID_W] = (
                    jnp.where(left, band(kr), band(kr + 1)))


def _na_kernel(rpb_ref, q_ref, k_ref, v_ref, z_ref, kc_ref, vc_ref, o_ref, band_s, bias_s, *, rows):
    nq = NA_QROWS * GRID_W
    nk = NA_KROWS * GRID_W
    n_blocks = rows // NA_QROWS
    scale = HEAD ** -0.5 * LOG2E
    kc = kc_ref[0, 0]
    vc = vc_ref[0, 0]
    nt = (((1,), (1,)), ((), ()))

    @pl.when(pl.program_id(1) == 0)
    def _():
        _build_bias(rpb_ref, pl.program_id(0), band_s, bias_s, rows)

    def scores(i):
        r0 = i * NA_QROWS
        kb = jnp.clip(r0 - NA_KH // 2, 0, rows - NA_KROWS)
        kind = jnp.where(i == 0, 0, jnp.where(i == n_blocks - 1, 2, 1))
        q_off = pl.multiple_of(r0 * GRID_W, GRID_W)
        k_off = pl.multiple_of(kb * GRID_W, GRID_W)
        q = (q_ref[0, 0, pl.ds(q_off, nq), :].astype(F32) * scale).astype(BF16)
        s_loc = lax.dot_general(q, k_ref[0, 0, pl.ds(k_off, nk), :], nt, preferred_element_type=F32)
        s_ctx = lax.dot_general(q, kc, nt, preferred_element_type=F32)
        return q_off, k_off, s_loc + bias_s[kind], s_ctx

    def attend(q_off, k_off, s_loc, s_ctx):
        m = jnp.maximum(jnp.max(s_loc, axis=-1, keepdims=True), jnp.max(s_ctx, axis=-1, keepdims=True))
        p_loc = jnp.exp2(s_loc - m)
        p_ctx = jnp.exp2(s_ctx - m)
        denom = jnp.sum(p_loc, axis=-1, keepdims=True) + jnp.sum(p_ctx, axis=-1, keepdims=True)
        o = (jnp.dot(p_loc.astype(BF16), v_ref[0, 0, pl.ds(k_off, nk), :], preferred_element_type=F32)
             + jnp.dot(p_ctx.astype(BF16), vc, preferred_element_type=F32))
        o = o / denom
        z = z_ref[0, 0, pl.ds(q_off, nq), :].astype(F32)
        o_ref[0, 0, pl.ds(q_off, nq), :] = (o * (z * jax.nn.sigmoid(z))).astype(BF16)

    def blocks(n, carry):
        staged = [scores(n * NA_UNROLL + u) for u in range(NA_UNROLL)]
        for args in staged:
            attend(*args)
        return carry

    lax.fori_loop(0, n_blocks // NA_UNROLL, blocks, 0)


def _na(proj, proj_ctx, rpb, nh):
    bsz, _, t, _ = proj.shape
    lc = proj_ctx.shape[2]
    rows = t // GRID_W
    assert rows % (NA_QROWS * NA_UNROLL) == 0 and rows >= NA_KROWS + NA_QROWS
    assert rpb.shape == (nh, N_DROW, N_DCOL)
    lat = lambda g: pl.BlockSpec((1, 1, t, HEAD), lambda h, b: (b, g * nh + h, 0, 0))
    ctx = lambda g: pl.BlockSpec((1, 1, lc, HEAD), lambda h, b: (g * nh + h, b, 0, 0))
    return pl.pallas_call(
        functools.partial(_na_kernel, rows=rows),
        out_shape=jax.ShapeDtypeStruct((bsz, nh, t, HEAD), BF16),
        grid=(nh, bsz),
        in_specs=[pl.BlockSpec(memory_space=pltpu.SMEM),
                  lat(0), lat(1), lat(2), lat(3), ctx(0), ctx(1)],
        out_specs=pl.BlockSpec((1, 1, t, HEAD), lambda h, b: (b, h, 0, 0)),
        scratch_shapes=[pltpu.VMEM((N_DROW, GRID_W, 2 * GRID_W), F32),
                        pltpu.VMEM((3, NA_QROWS * GRID_W, NA_KROWS * GRID_W), F32)],
        compiler_params=_params(("arbitrary", "arbitrary")),
        name="na",
    )(rpb.astype(F32).reshape(-1), proj, proj, proj, proj, proj_ctx, proj_ctx)


def _seg_cumsum(x, reverse):
    r_i = lax.broadcasted_iota(jnp.int32, (CHUNK, 2 * CHUNK), 0)
    c_i = lax.broadcasted_iota(jnp.int32, (CHUNK, 2 * CHUNK), 1) & (CHUNK - 1)
    tri2 = jnp.where((r_i <= c_i) if reverse else (r_i >= c_i), 1.0, 0.0).astype(BF16)
    hi = x.astype(BF16)
    lo = (x - hi.astype(F32)).astype(BF16)
    out = []
    for c in range(x.shape[0] // CHUNK):
        rows = slice(c * CHUNK, (c + 1) * CHUNK)
        out.append(jnp.dot(tri2, jnp.concatenate([hi[rows], lo[rows]], axis=0), preferred_element_type=F32))
    return jnp.concatenate(out, axis=0)


def _forget(f_pre, lb):
    sg = jax.nn.sigmoid(f_pre)
    one_m_lb = 1.0 - lb
    return one_m_lb * (1.0 - sg), jnp.log(lb + one_m_lb * sg)


def _lower_bound(logits):
    m = jnp.max(logits, axis=0, keepdims=True)
    e = jnp.exp(logits - m)
    return e[0:1] / jnp.sum(e, axis=0, keepdims=True)


def _gates(f_pre, lb, reverse):
    kk, logf = _forget(f_pre, lb)
    cum = _seg_cumsum(logf, reverse).reshape(-1, CHUNK, HEAD)
    mid = CHUNK // 2
    ref = cum[:, mid:mid + 1] if reverse else cum[:, mid - 1:mid]
    last = cum[:, 0:1] if reverse else cum[:, CHUNK - 1:CHUNK]
    return kk.reshape(cum.shape), cum, ref, last


def _decay_columns(dec):
    pad = jnp.zeros((HEAD - dec.shape[0], HEAD), F32)
    return jnp.concatenate([dec, pad], axis=0).T


def _ctx_state(f_pre, v, lb, reverse):
    kk, cum, _, last = _gates(f_pre, lb, reverse)
    kl = (kk * jnp.exp(last - cum)).astype(BF16)
    dec_t = _decay_columns(jnp.exp(last).reshape(-1, HEAD))
    tn = (((0,), (0,)), ((), ()))
    n_chunks = kl.shape[0]
    st = jnp.zeros((HEAD, HEAD), F32)
    for c in (reversed(range(n_chunks)) if reverse else range(n_chunks)):
        u = lax.dot_general(kl[c], v[c * CHUNK:(c + 1) * CHUNK], tn, preferred_element_type=F32)
        st = st * dec_t[:, c:c + 1] + u
    return st


def _hg_prepare(qs, f_pre, lb, reverse):
    kk, cum, ref, last = _gates(f_pre, lb, reverse)
    e = jnp.exp(cum - ref)
    qd = qs.reshape(cum.shape) * e
    kd = kk / e
    qe = qd * jnp.exp(ref)
    kl = kd * jnp.exp(last - ref)
    flat = lambda a: a.reshape(-1, HEAD).astype(BF16)
    return flat(qd), flat(kd), flat(qe), flat(kl), jnp.exp(last).reshape(-1, HEAD)


def _hg_kernel(q_ref, ff_ref, fb_ref, i_ref, g_ref, ffc_ref, fbc_ref, ic_ref, lbf_ref, lbb_ref, nw_ref,
               o_ref, qs_s, qd_s, kd_s, qe_s, kl_s, dec_s, ob_s, *, t):
    n_tiles = t // HG_TILE
    lb_f = _lower_bound(lbf_ref[...])
    lb_b = _lower_bound(lbb_ref[...])
    nw = nw_ref[...]
    r_i = lax.broadcasted_iota(jnp.int32, (CHUNK, CHUNK), 0)
    c_i = lax.broadcasted_iota(jnp.int32, (CHUNK, CHUNK), 1)
    nt = (((1,), (1,)), ((), ()))
    tn = (((0,), (0,)), ((), ()))

    def prepare(tix, f_ref, lb, reverse, first):
        rs = pl.ds(pl.multiple_of(tix * HG_TILE, HG_TILE), HG_TILE)
        if first:
            q = q_ref[0, 0, rs, :].astype(F32)
            qs = q * jax.nn.sigmoid(q)
        else:
            qs = qs_s[rs, :]
        return (qs if first else None,) + _hg_prepare(qs, f_ref[0, 0, rs, :].astype(F32), lb, reverse)

    def stage(tix, vals):
        rs = pl.ds(pl.multiple_of(tix * HG_TILE, HG_TILE), HG_TILE)
        qs, qd, kd, qe, kl, dec = vals
        if qs is not None:
            qs_s[rs, :] = qs
        qd_s[rs, :] = qd
        kd_s[rs, :] = kd
        qe_s[rs, :] = qe
        kl_s[rs, :] = kl
        dec_s[tix] = dec

    def scan(tix, st, reverse, emit):
        tri = (r_i <= c_i) if reverse else (r_i >= c_i)
        order = [HG_CHUNKS - 1 - u if reverse else u for u in range(HG_CHUNKS)]
        cs = {j: pl.ds(pl.multiple_of(tix * HG_TILE + j * CHUNK, CHUNK), CHUNK) for j in order}
        v = {j: i_ref[0, 0, cs[j], :] for j in order}
        a = {j: lax.dot_general(qd_s[cs[j], :], kd_s[cs[j], :], nt, preferred_element_type=F32) for j in order}
        u = {j: lax.dot_general(kl_s[cs[j], :], v[j], tn, preferred_element_type=F32) for j in order}
        dec_t = _decay_columns(dec_s[tix])
        dec_b = {j: jnp.broadcast_to(dec_t[:, j:j + 1], (HEAD, HEAD)) for j in order}
        for j in order:
            lhs = jnp.concatenate([qe_s[cs[j], :], jnp.where(tri, a[j], 0.0).astype(BF16)], axis=1)
            rhs = jnp.concatenate([st.astype(BF16), v[j]], axis=0)
            emit(cs[j], jnp.dot(lhs, rhs, preferred_element_type=F32))
            st = st * dec_b[j] + u[j]
        return st

    def direction(f_ref, fc_ref, lb, reverse, first, emit):
        tile_of = (lambda n: n_tiles - 1 - n) if reverse else (lambda n: n)
        stage(tile_of(0), prepare(tile_of(0), f_ref, lb, reverse, first))
        st = _ctx_state(fc_ref[0, 0].astype(F32), ic_ref[0, 0], lb, reverse)

        def trip(n, st):
            nxt = tile_of(jnp.minimum(n + 1, n_tiles - 1))
            vals = prepare(nxt, f_ref, lb, reverse, first)
            st = scan(tile_of(n), st, reverse, emit)
            stage(nxt, vals)
            return st

        lax.fori_loop(0, n_tiles, trip, st)

    def emit_backward(cs, o):
        ob_s[cs, :] = o

    def emit_forward(cs, o):
        o = o + ob_s[cs, :]
        o = o * lax.rsqrt(jnp.mean(o * o, axis=-1, keepdims=True) + RMS_EPS) * nw
        g = g_ref[0, 0, cs, :].astype(F32)
        o_ref[0, 0, cs, :] = (o * (g * jax.nn.sigmoid(g))).astype(BF16)

    direction(fb_ref, fbc_ref, lb_b, True, True, emit_backward)
    direction(ff_ref, ffc_ref, lb_f, False, False, emit_forward)


def _hgrn2(proj, proj_ctx, lb_f, lb_b, norm_w, nh):
    bsz, _, t, _ = proj.shape
    lc = proj_ctx.shape[2]
    n_slots = lb_f.shape[0]
    assert t % HG_TILE == 0 and lc % CHUNK == 0
    lat = lambda g: pl.BlockSpec((1, 1, t, HEAD), lambda b, h: (b, g * nh + h, 0, 0))
    ctx = lambda g: pl.BlockSpec((1, 1, lc, HEAD), lambda b, h: (g * nh + h, b, 0, 0))
    lbs = pl.BlockSpec((n_slots, HEAD), lambda b, h: (0, h))
    staged = pltpu.VMEM((t, HEAD), BF16)
    return pl.pallas_call(
        functools.partial(_hg_kernel, t=t),
        out_shape=jax.ShapeDtypeStruct((bsz, nh, t, HEAD), BF16),
        grid=(bsz, nh),
        in_specs=[lat(4), lat(5), lat(6), lat(7), lat(8), ctx(2), ctx(3), ctx(4), lbs, lbs,
                  pl.BlockSpec((1, HEAD), lambda b, h: (0, 0))],
        out_specs=pl.BlockSpec((1, 1, t, HEAD), lambda b, h: (b, h, 0, 0)),
        scratch_shapes=[pltpu.VMEM((t, HEAD), F32), staged, staged, staged, staged,
                        pltpu.VMEM((t // HG_TILE, HG_CHUNKS, HEAD), F32), pltpu.VMEM((t, HEAD), F32)],
        compiler_params=_params(("arbitrary", "arbitrary")),
        name="hgrn2",
    )(proj, proj, proj, proj, proj, proj_ctx, proj_ctx, proj_ctx, lb_f, lb_b, norm_w)


def _merge_kernel(ya_ref, yb_ref, wa_ref, wb_ref, ga_ref, gb_ref, o_ref, ya_s, yb_s):
    @pl.when(pl.program_id(2) == 0)
    def _():
        for h in range(ya_ref.shape[1]):
            ya_s[:, h * HEAD:(h + 1) * HEAD] = ya_ref[0, h]
            yb_s[:, h * HEAD:(h + 1) * HEAD] = yb_ref[0, h]

    ta = jnp.dot(ya_s[...], wa_ref[...], preferred_element_type=F32)
    tb = jnp.dot(yb_s[...], wb_ref[...], preferred_element_type=F32)
    for s in range(ga_ref.shape[1]):
        cs = slice(s * HEAD, (s + 1) * HEAD)
        m = (jax.nn.sigmoid(ga_ref[0, s].astype(F32)) * ta[:, cs]
             + jax.nn.sigmoid(gb_ref[0, s].astype(F32)) * tb[:, cs])
        o_ref[0, :, cs] = m.astype(BF16)


def _merge(ya, yb, w_pa, w_pb, proj, nh):
    bsz, _, t, _ = ya.shape
    d = w_pa.shape[1]
    tm = _tile(t, 512)
    tn = _tile(nh * HEAD, 1024)
    sub = tn // HEAD
    ga0 = 9 * nh // sub
    gb0 = 11 * nh // sub
    assert (9 * nh) % sub == 0 and (11 * nh) % sub == 0
    yspec = pl.BlockSpec((1, nh, tm, HEAD), lambda b, i, j: (b, 0, i, 0))
    wspec = pl.BlockSpec((nh * HEAD, tn), lambda b, i, j: (0, j))
    return pl.pallas_call(
        _merge_kernel,
        out_shape=jax.ShapeDtypeStruct((bsz, t, d), BF16),
        grid=(bsz, t // tm, d // tn),
        in_specs=[yspec, yspec, wspec, wspec,
                  pl.BlockSpec((1, sub, tm, HEAD), lambda b, i, j: (b, ga0 + j, i, 0)),
                  pl.BlockSpec((1, sub, tm, HEAD), lambda b, i, j: (b, gb0 + j, i, 0))],
        out_specs=pl.BlockSpec((1, tm, tn), lambda b, i, j: (b, i, j)),
        scratch_shapes=[pltpu.VMEM((tm, nh * HEAD), BF16), pltpu.VMEM((tm, nh * HEAD), BF16)],
        compiler_params=_params(("arbitrary", "arbitrary", "arbitrary")),
        name="merge",
    )(ya, yb, w_pa, w_pb, proj, proj)


def _out_kernel(m_ref, w_ref, x_ref, gate_ref, lng_ref, lnb_ref, o_ref, *, alpha, tn):
    j = pl.program_id(2)
    out = jnp.dot(m_ref[0], w_ref[...], preferred_element_type=F32)
    col = pl.multiple_of(j * tn, tn)
    o_ref[0, :, pl.ds(col, tn)] = alpha * x_ref[0] + gate_ref[0] * out

    @pl.when(j == pl.num_programs(2) - 1)
    def _():
        def norm_rows(i, carry):
            rs = pl.ds(pl.multiple_of(i * LN_ROWS, LN_ROWS), LN_ROWS)
            r = o_ref[0, rs, :]
            mu = jnp.mean(r, axis=-1, keepdims=True)
            rc = r - mu
            var = jnp.mean(rc * rc, axis=-1, keepdims=True)
            o_ref[0, rs, :] = rc * lax.rsqrt(var + LN_EPS) * lng_ref[...] + lnb_ref[...]
            return carry

        lax.fori_loop(0, o_ref.shape[1] // LN_ROWS, norm_rows, 0)


def _out(m, w_out, x, gate, ln_g, ln_b, alpha):
    bsz, t, d = x.shape
    tm = _tile(t, 512)
    tn = _tile(d, 1024)
    return pl.pallas_call(
        functools.partial(_out_kernel, alpha=alpha, tn=tn),
        out_shape=jax.ShapeDtypeStruct((bsz, t, d), F32),
        grid=(bsz, t // tm, d // tn),
        in_specs=[pl.BlockSpec((1, tm, d), lambda b, i, j: (b, i, 0)),
                  pl.BlockSpec((d, tn), lambda b, i, j: (0, j)),
                  pl.BlockSpec((1, tm, tn), lambda b, i, j: (b, i, j)),
                  pl.BlockSpec((1, 1, tn), lambda b, i, j: (b, 0, j)),
                  pl.BlockSpec((1, d), lambda b, i, j: (0, 0)),
                  pl.BlockSpec((1, d), lambda b, i, j: (0, 0))],
        out_specs=pl.BlockSpec((1, tm, d), lambda b, i, j: (b, i, 0)),
        compiler_params=_params(("arbitrary", "arbitrary", "arbitrary")),
        name="out",
    )(m, w_out, x, gate, ln_g, ln_b)


def _layer(x, ctx, cond, w_ada, b_ada, w_in, rpb, lb_f, lb_b, norm_w, w_pa, w_pb, w_out, ln_g, ln_b, alpha):
    bsz, t, d = x.shape
    nh = (d // 2) // HEAD
    grp = d // 2
    tn = _tile(grp, 1024)
    per = grp // tn

    mod = _adaln(cond, w_ada, b_ada.reshape(1, -1))
    shift, scale, gate = (mod[:3, k * d:(k + 1) * d].reshape(3, 1, d) for k in range(3))
    ctx_row = bsz

    h_lat = _ln_mod(x, shift, scale, lambda b: b)
    h_ctx = _ln_mod(ctx, shift, scale, lambda b: ctx_row)

    first, w_in16 = _proj_first(h_lat, w_in)
    proj, w_pa16, w_pb16, w_out16 = _proj(h_lat, w_in16, 13 * per, lambda j: j, tn, "proj_lat", first=first,
                                          cast=(w_pa, w_pb, w_out))
    proj_ctx = _proj(h_ctx.reshape(1, -1, d), w_in16, 5 * per,
                     lambda j: jnp.where(j < 2 * per, j + per, j + 3 * per), tn, "proj_ctx")
    proj_ctx = proj_ctx.reshape(5 * nh, bsz, ctx.shape[1], HEAD)

    y_a = _na(proj, proj_ctx, rpb, nh)
    y_b = _hgrn2(proj, proj_ctx, lb_f, lb_b, norm_w.reshape(1, HEAD), nh)
    m = _merge(y_a, y_b, w_pa16, w_pb16, proj, nh)
    return _out(m, w_out16, x, gate, ln_g.reshape(1, d), ln_b.reshape(1, d), alpha)


def kernel(x, c, ctx, c_ctx, w_ada, b_ada, w_in, na_rpb, hg_lb_fwd, hg_lb_bwd, hg_norm_w, w_pa, w_pb, w_out,
           ln_g, ln_b):
    depth = w_ada.shape[0]
    assert depth == 1, "the context stream update of deeper stacks is not implemented"
    bsz, _, d = x.shape
    alpha = (2.0 * depth) ** 0.25
    cond = jnp.concatenate([c, c_ctx[None], jnp.zeros((8 - bsz - 1, d), c.dtype)], axis=0)
    return _layer(x, ctx, cond, w_ada[0], b_ada[0], w_in[0], na_rpb[0], hg_lb_fwd, hg_lb_bwd, hg_norm_w[0],
                  w_pa[0], w_pb[0], w_out[0], ln_g[0], ln_b[0], alpha)
```

```python
import functools
import math

import jax
import jax.numpy as jnp
from jax import lax
from jax.experimental import pallas as pl
from jax.experimental.pallas import tpu as pltpu

F32 = jnp.float32
BF16 = jnp.bfloat16

GRID_W = 64
HEAD = 128
NA_KH = 8
NA_KW = 16
NA_QROWS = 4
NA_KROWS = 12
NA_UNROLL = 4
CHUNK = 64
HG_CHUNKS = 16
HG_TILE = HG_CHUNKS * CHUNK
LN_ROWS = 64
LOG2E = math.log2(math.e)
LN_EPS = 1e-6
RMS_EPS = 1e-6
VMEM_LIMIT = 56 * 1024 * 1024


def _params(semantics, vmem=VMEM_LIMIT):
    return pltpu.CompilerParams(dimension_semantics=semantics, vmem_limit_bytes=vmem)


def _tile(n, pref):
    t = min(n, pref)
    assert n % t == 0, (n, t)
    return t


def _adaln_kernel(c_ref, w_ref, b_ref, o_ref):
    c = c_ref[...]
    s = c * jax.nn.sigmoid(c)
    s_hi = s.astype(BF16)
    s_lo = (s - s_hi.astype(F32)).astype(BF16)
    w = w_ref[...]
    w_hi = w.astype(BF16)
    w_lo = (w - w_hi.astype(F32)).astype(BF16)
    n = s.shape[0]
    lhs = jnp.concatenate([s_hi, s_lo], axis=0)
    a = jnp.dot(lhs, w_hi, preferred_element_type=F32)
    b = jnp.dot(s_hi, w_lo, preferred_element_type=F32)
    o_ref[...] = a[:n] + a[n:] + b + b_ref[...]


def _adaln(cond, w, b):
    d, n = w.shape
    tn = _tile(n, 512)
    return pl.pallas_call(
        _adaln_kernel,
        out_shape=jax.ShapeDtypeStruct((cond.shape[0], n), F32),
        grid=(n // tn,),
        in_specs=[pl.BlockSpec(cond.shape, lambda j: (0, 0)),
                  pl.BlockSpec((d, tn), lambda j: (0, j)),
                  pl.BlockSpec((1, tn), lambda j: (0, j))],
        out_specs=pl.BlockSpec((cond.shape[0], tn), lambda j: (0, j)),
        compiler_params=_params(("arbitrary",)),
        name="adaln",
    )(cond, w, b)


def _ln_mod_kernel(x_ref, shift_ref, scale_ref, o_ref):
    x = x_ref[0]
    mu = jnp.mean(x, axis=-1, keepdims=True)
    xc = x - mu
    var = jnp.mean(xc * xc, axis=-1, keepdims=True)
    y = xc * lax.rsqrt(var + LN_EPS)
    o_ref[0] = (y * (1.0 + scale_ref[0]) + shift_ref[0]).astype(BF16)


def _ln_mod(x, shift, scale, cond_row):
    bsz, t, d = x.shape
    tm = _tile(t, 512)
    cmap = lambda b, i: (cond_row(b), 0, 0)
    return pl.pallas_call(
        _ln_mod_kernel,
        out_shape=jax.ShapeDtypeStruct((bsz, t, d), BF16),
        grid=(bsz, t // tm),
        in_specs=[pl.BlockSpec((1, tm, d), lambda b, i: (b, i, 0)),
                  pl.BlockSpec((1, 1, d), cmap),
                  pl.BlockSpec((1, 1, d), cmap)],
        out_specs=pl.BlockSpec((1, tm, d), lambda b, i: (b, i, 0)),
        compiler_params=_params(("arbitrary", "arbitrary")),
        name="ln_mod",
    )(x, shift, scale)


def _store_head_major(o_ref, acc):
    for s in range(o_ref.shape[1]):
        o_ref[0, s] = acc[:, s * HEAD:(s + 1) * HEAD].astype(BF16)


def _proj_kernel(h_ref, w_ref, o_ref):
    _store_head_major(o_ref, jnp.dot(h_ref[0], w_ref[...], preferred_element_type=F32))


def _proj_cast_kernel(h_ref, w_ref, o_ref, w16_ref):
    w16 = w_ref[...].astype(BF16)
    w16_ref[...] = w16
    _store_head_major(o_ref, jnp.dot(h_ref[0], w16, preferred_element_type=F32))


def _proj_first(h, w):
    _, t, d = h.shape
    n = w.shape[1]
    tm = _tile(t, 1024)
    tn = next(c for c in (512, 256, HEAD) if n % c == 0)
    sub = tn // HEAD
    return pl.pallas_call(
        _proj_cast_kernel,
        out_shape=(jax.ShapeDtypeStruct((1, n // HEAD, tm, HEAD), BF16), jax.ShapeDtypeStruct((d, n), BF16)),
        grid=(n // tn,),
        in_specs=[pl.BlockSpec((1, tm, d), lambda j: (0, 0, 0)),
                  pl.BlockSpec((d, tn), lambda j: (0, j))],
        out_specs=[pl.BlockSpec((1, sub, tm, HEAD), lambda j: (0, j, 0, 0)),
                   pl.BlockSpec((d, tn), lambda j: (0, j))],
        compiler_params=_params(("arbitrary",)),
        name="proj_first",
    )(h, w)


def _proj(h, w, n_tiles, col_tile, tn, name):
    bsz, t, d = h.shape
    tm = _tile(t, 1024)
    sub = tn // HEAD
    return pl.pallas_call(
        _proj_kernel,
        out_shape=jax.ShapeDtypeStruct((bsz, n_tiles * sub, t, HEAD), BF16),
        grid=(bsz, t // tm, n_tiles),
        in_specs=[pl.BlockSpec((1, tm, d), lambda b, i, j: (b, i, 0)),
                  pl.BlockSpec((d, tn), lambda b, i, j: (0, col_tile(j)))],
        out_specs=pl.BlockSpec((1, sub, tm, HEAD), lambda b, i, j: (b, j, i, 0)),
        compiler_params=_params(("arbitrary", "arbitrary", "arbitrary")),
        name=name,
    )(h, w)


def _proj_main_kernel(h_ref, w_ref, first_ref, *refs, n_cast, tm):
    o_ref = refs[n_cast]
    rp = pl.program_id(1)

    @pl.when(rp == 0)
    def _():
        o_ref[0, :, 0:tm, :] = first_ref[0]

    acc = jnp.dot(h_ref[0], w_ref[...], preferred_element_type=F32)
    rows = pl.ds(pl.multiple_of(((rp + 1) % 2) * tm, tm), tm)
    for s in range(o_ref.shape[1]):
        o_ref[0, s, rows, :] = acc[:, s * HEAD:(s + 1) * HEAD].astype(BF16)
    for src, dst in zip(refs[:n_cast], refs[n_cast + 1:]):
        dst[...] = src[...].astype(BF16)


def _proj_main(h, w, first, tn, cast):
    bsz, t, d = h.shape
    n = w.shape[1]
    tm = _tile(t, 1024)
    nb = t // tm
    sub = tn // HEAD
    n_tiles = n // tn
    n_rows = bsz * nb - 1
    assert nb % 2 == 0 and first.shape == (1, n // HEAD, tm, HEAD)
    blk = lambda rp: ((rp + 1) // nb, (rp + 1) % nb)
    in_specs = [pl.BlockSpec((1, tm, d), lambda j, rp: (*blk(rp), 0)),
                pl.BlockSpec((d, tn), lambda j, rp: (0, j)),
                pl.BlockSpec((1, sub, tm, HEAD), lambda j, rp: (0, j, 0, 0))]
    out_specs = [pl.BlockSpec((1, sub, 2 * tm, HEAD), lambda j, rp: (blk(rp)[0], j, blk(rp)[1] // 2, 0))]
    out_shape = [jax.ShapeDtypeStruct((bsz, n // HEAD, t, HEAD), BF16)]
    for cw in cast:
        rows, cols = cw.shape
        split = next(s for s in range(n_rows * n_tiles, 0, -1) if rows % s == 0 and (rows // s) % 16 == 0)
        spec = pl.BlockSpec((rows // split, cols),
                            lambda j, rp, split=split: (jnp.minimum(j * n_rows + rp, split - 1), 0))
        in_specs.append(spec)
        out_specs.append(spec)
        out_shape.append(jax.ShapeDtypeStruct(cw.shape, BF16))
    return pl.pallas_call(
        functools.partial(_proj_main_kernel, n_cast=len(cast), tm=tm),
        out_shape=out_shape,
        grid=(n_tiles, n_rows),
        in_specs=in_specs,
        out_specs=out_specs,
        compiler_params=_params(("arbitrary", "arbitrary")),
        name="proj_lat",
    )(h, w, first, *cast)


N_DROW = 2 * NA_KH - 1
N_DCOL = 2 * NA_KW - 1


def _na_block_kinds(rows):
    return ((0, 0), (NA_QROWS, 0), (rows - NA_QROWS, rows - NA_KROWS))


def _build_bias(rpb_ref, head, band_s, bias_s, rows):
    shape = (GRID_W, 2 * GRID_W)
    qc = lax.broadcasted_iota(jnp.int32, shape, 0)
    lane = lax.broadcasted_iota(jnp.int32, shape, 1)
    kc = lane & (GRID_W - 1)
    d_col = jnp.clip(kc - qc, -(NA_KW - 1), NA_KW - 1) + (NA_KW - 1)
    col_start = jnp.clip(qc - NA_KW // 2, 0, GRID_W - NA_KW)
    col_off = kc - col_start
    neg = jnp.full(shape, -jnp.inf, F32)
    base = head * (N_DROW * N_DCOL)
    for d in range(N_DROW):
        acc = neg
        for j in range(N_DCOL):
            acc = jnp.where(d_col == j, rpb_ref[base + d * N_DCOL + j] * LOG2E, acc)
        band_s[d] = jnp.where(col_off < 0, neg, jnp.where(col_off < NA_KW, acc, neg))
    left = lane < GRID_W
    for kind, (r0, kb) in enumerate(_na_block_kinds(rows)):
        for qi in range(NA_QROWS):
            r = r0 + qi
            row_start = min(max(r - NA_KH // 2, 0), rows - NA_KH)

            def band(kr):
                return band_s[kr - r + NA_KH - 1] if row_start <= kr < row_start + NA_KH else neg

            for jp in range(NA_KROWS // 2):
                kr = kb + 2 * jp
                bias_s[kind, qi * GRID_W:(qi + 1) * GRID_W, jp * 2 * GRID_W:(jp + 1) * 2 * GRID_W] = (
                    jnp.where(left, band(kr), band(kr + 1)))


def _na_kernel(rpb_ref, q_ref, k_ref, v_ref, z_ref, kc_ref, vc_ref, o_ref, band_s, bias_s, *, rows):
    nq = NA_QROWS * GRID_W
    nk = NA_KROWS * GRID_W
    n_blocks = rows // NA_QROWS
    scale = HEAD ** -0.5 * LOG2E
    kc = kc_ref[0, 0]
    vc = vc_ref[0, 0]
    nt = (((1,), (1,)), ((), ()))

    @pl.when(pl.program_id(1) == 0)
    def _():
        _build_bias(rpb_ref, pl.program_id(0), band_s, bias_s, rows)

    def scores(i):
        r0 = i * NA_QROWS
        kb = jnp.clip(r0 - NA_KH // 2, 0, rows - NA_KROWS)
        kind = jnp.where(i == 0, 0, jnp.where(i == n_blocks - 1, 2, 1))
        q_off = pl.multiple_of(r0 * GRID_W, GRID_W)
        k_off = pl.multiple_of(kb * GRID_W, GRID_W)
        q = (q_ref[0, 0, pl.ds(q_off, nq), :].astype(F32) * scale).astype(BF16)
        s_loc = lax.dot_general(q, k_ref[0, 0, pl.ds(k_off, nk), :], nt, preferred_element_type=F32)
        s_ctx = lax.dot_general(q, kc, nt, preferred_element_type=F32)
        return q_off, k_off, s_loc + bias_s[kind], s_ctx

    def attend(q_off, k_off, s_loc, s_ctx):
        m = jnp.maximum(jnp.max(s_loc, axis=-1, keepdims=True), jnp.max(s_ctx, axis=-1, keepdims=True))
        p_loc = jnp.exp2(s_loc - m)
        p_ctx = jnp.exp2(s_ctx - m)
        denom = jnp.sum(p_loc, axis=-1, keepdims=True) + jnp.sum(p_ctx, axis=-1, keepdims=True)
        o = (jnp.dot(p_loc.astype(BF16), v_ref[0, 0, pl.ds(k_off, nk), :], preferred_element_type=F32)
             + jnp.dot(p_ctx.astype(BF16), vc, preferred_element_type=F32))
        o = o / denom
        z = z_ref[0, 0, pl.ds(q_off, nq), :].astype(F32)
        o_ref[0, 0, pl.ds(q_off, nq), :] = (o * (z * jax.nn.sigmoid(z))).astype(BF16)

    def blocks(n, carry):
        staged = [scores(n * NA_UNROLL + u) for u in range(NA_UNROLL)]
        for args in staged:
            attend(*args)
        return carry

    lax.fori_loop(0, n_blocks // NA_UNROLL, blocks, 0)


def _na(proj, proj_ctx, rpb, nh):
    bsz, _, t, _ = proj.shape
    lc = proj_ctx.shape[2]
    rows = t // GRID_W
    assert rows % (NA_QROWS * NA_UNROLL) == 0 and rows >= NA_KROWS + NA_QROWS
    assert rpb.shape == (nh, N_DROW, N_DCOL)
    lat = lambda g: pl.BlockSpec((1, 1, t, HEAD), lambda h, b: (b, g * nh + h, 0, 0))
    ctx = lambda g: pl.BlockSpec((1, 1, lc, HEAD), lambda h, b: (g * nh + h, b, 0, 0))
    return pl.pallas_call(
        functools.partial(_na_kernel, rows=rows),
        out_shape=jax.ShapeDtypeStruct((bsz, nh, t, HEAD), BF16),
        grid=(nh, bsz),
        in_specs=[pl.BlockSpec(memory_space=pltpu.SMEM),
                  lat(0), lat(1), lat(2), lat(3), ctx(0), ctx(1)],
        out_specs=pl.BlockSpec((1, 1, t, HEAD), lambda h, b: (b, h, 0, 0)),
        scratch_shapes=[pltpu.VMEM((N_DROW, GRID_W, 2 * GRID_W), F32),
                        pltpu.VMEM((3, NA_QROWS * GRID_W, NA_KROWS * GRID_W), F32)],
        compiler_params=_params(("arbitrary", "arbitrary")),
        name="na",
    )(rpb.astype(F32).reshape(-1), proj, proj, proj, proj, proj_ctx, proj_ctx)


def _seg_cumsum(x, reverse):
    r_i = lax.broadcasted_iota(jnp.int32, (CHUNK, 2 * CHUNK), 0)
    c_i = lax.broadcasted_iota(jnp.int32, (CHUNK, 2 * CHUNK), 1) & (CHUNK - 1)
    tri2 = jnp.where((r_i <= c_i) if reverse else (r_i >= c_i), 1.0, 0.0).astype(BF16)
    hi = x.astype(BF16)
    lo = (x - hi.astype(F32)).astype(BF16)
    out = []
    for c in range(x.shape[0] // CHUNK):
        rows = slice(c * CHUNK, (c + 1) * CHUNK)
        out.append(jnp.dot(tri2, jnp.concatenate([hi[rows], lo[rows]], axis=0), preferred_element_type=F32))
    return jnp.concatenate(out, axis=0)


def _forget(f_pre, lb):
    sg = jax.nn.sigmoid(f_pre)
    one_m_lb = 1.0 - lb
    return one_m_lb * (1.0 - sg), jnp.log(lb + one_m_lb * sg)


def _lower_bound(logits):
    m = jnp.max(logits, axis=0, keepdims=True)
    e = jnp.exp(logits - m)
    return e[0:1] / jnp.sum(e, axis=0, keepdims=True)


def _gates(f_pre, lb, reverse):
    kk, logf = _forget(f_pre, lb)
    cum = _seg_cumsum(logf, reverse).reshape(-1, CHUNK, HEAD)
    mid = CHUNK // 2
    ref = cum[:, mid:mid + 1] if reverse else cum[:, mid - 1:mid]
    last = cum[:, 0:1] if reverse else cum[:, CHUNK - 1:CHUNK]
    return kk.reshape(cum.shape), cum, ref, last


def _decay_columns(dec):
    pad = jnp.zeros((HEAD - dec.shape[0], HEAD), F32)
    return jnp.concatenate([dec, pad], axis=0).T


def _ctx_state(f_pre, v, lb, reverse):
    kk, cum, _, last = _gates(f_pre, lb, reverse)
    kl = (kk * jnp.exp(last - cum)).astype(BF16)
    dec_t = _decay_columns(jnp.exp(last).reshape(-1, HEAD))
    tn = (((0,), (0,)), ((), ()))
    n_chunks = kl.shape[0]
    st = jnp.zeros((HEAD, HEAD), F32)
    for c in (reversed(range(n_chunks)) if reverse else range(n_chunks)):
        u = lax.dot_general(kl[c], v[c * CHUNK:(c + 1) * CHUNK], tn, preferred_element_type=F32)
        st = st * dec_t[:, c:c + 1] + u
    return st


def _hg_prepare(qs, f_pre, lb, reverse):
    kk, cum, ref, last = _gates(f_pre, lb, reverse)
    e = jnp.exp(cum - ref)
    qd = qs.reshape(cum.shape) * e
    kd = kk / e
    qe = qd * jnp.exp(ref)
    kl = kd * jnp.exp(last - ref)
    flat = lambda a: a.reshape(-1, HEAD).astype(BF16)
    return flat(qd), flat(kd), flat(qe), flat(kl), jnp.exp(last).reshape(-1, HEAD)


def _hg_kernel(q_ref, ff_ref, fb_ref, i_ref, g_ref, ffc_ref, fbc_ref, ic_ref, lbf_ref, lbb_ref, nw_ref,
               o_ref, qs_s, qd_s, kd_s, qe_s, kl_s, dec_s, ob_s, *, t):
    n_tiles = t // HG_TILE
    lb_f = _lower_bound(lbf_ref[...])
    lb_b = _lower_bound(lbb_ref[...])
    nw = nw_ref[...]
    r_i = lax.broadcasted_iota(jnp.int32, (CHUNK, CHUNK), 0)
    c_i = lax.broadcasted_iota(jnp.int32, (CHUNK, CHUNK), 1)
    nt = (((1,), (1,)), ((), ()))
    tn = (((0,), (0,)), ((), ()))

    def prepare(tix, f_ref, lb, reverse, first):
        rs = pl.ds(pl.multiple_of(tix * HG_TILE, HG_TILE), HG_TILE)
        if first:
            q = q_ref[0, 0, rs, :].astype(F32)
            qs = q * jax.nn.sigmoid(q)
        else:
            qs = qs_s[rs, :]
        return (qs if first else None,) + _hg_prepare(qs, f_ref[0, 0, rs, :].astype(F32), lb, reverse)

    def stage(tix, vals):
        rs = pl.ds(pl.multiple_of(tix * HG_TILE, HG_TILE), HG_TILE)
        qs, qd, kd, qe, kl, dec = vals
        if qs is not None:
            qs_s[rs, :] = qs
        qd_s[rs, :] = qd
        kd_s[rs, :] = kd
        qe_s[rs, :] = qe
        kl_s[rs, :] = kl
        dec_s[tix] = dec

    def scan(tix, st, reverse, emit):
        tri = (r_i <= c_i) if reverse else (r_i >= c_i)
        order = [HG_CHUNKS - 1 - u if reverse else u for u in range(HG_CHUNKS)]
        cs = {j: pl.ds(pl.multiple_of(tix * HG_TILE + j * CHUNK, CHUNK), CHUNK) for j in order}
        v = {j: i_ref[0, 0, cs[j], :] for j in order}
        a = {j: lax.dot_general(qd_s[cs[j], :], kd_s[cs[j], :], nt, preferred_element_type=F32) for j in order}
        u = {j: lax.dot_general(kl_s[cs[j], :], v[j], tn, preferred_element_type=F32) for j in order}
        dec_t = _decay_columns(dec_s[tix])
        dec_b = {j: jnp.broadcast_to(dec_t[:, j:j + 1], (HEAD, HEAD)) for j in order}
        for j in order:
            lhs = jnp.concatenate([qe_s[cs[j], :], jnp.where(tri, a[j], 0.0).astype(BF16)], axis=1)
            rhs = jnp.concatenate([st.astype(BF16), v[j]], axis=0)
            emit(cs[j], jnp.dot(lhs, rhs, preferred_element_type=F32))
            st = st * dec_b[j] + u[j]
        return st

    def direction(f_ref, fc_ref, lb, reverse, first, emit):
        tile_of = (lambda n: n_tiles - 1 - n) if reverse else (lambda n: n)
        stage(tile_of(0), prepare(tile_of(0), f_ref, lb, reverse, first))
        st = _ctx_state(fc_ref[0, 0].astype(F32), ic_ref[0, 0], lb, reverse)

        def trip(n, st):
            nxt = tile_of(jnp.minimum(n + 1, n_tiles - 1))
            vals = prepare(nxt, f_ref, lb, reverse, first)
            st = scan(tile_of(n), st, reverse, emit)
            stage(nxt, vals)
            return st

        lax.fori_loop(0, n_tiles, trip, st)

    def emit_backward(cs, o):
        ob_s[cs, :] = o

    def emit_forward(cs, o):
        o = o + ob_s[cs, :]
        o = o * lax.rsqrt(jnp.mean(o * o, axis=-1, keepdims=True) + RMS_EPS) * nw
        g = g_ref[0, 0, cs, :].astype(F32)
        o_ref[0, 0, cs, :] = (o * (g * jax.nn.sigmoid(g))).astype(BF16)

    direction(fb_ref, fbc_ref, lb_b, True, True, emit_backward)
    direction(ff_ref, ffc_ref, lb_f, False, False, emit_forward)


def _hgrn2(proj, proj_ctx, lb_f, lb_b, norm_w, nh):
    bsz, _, t, _ = proj.shape
    lc = proj_ctx.shape[2]
    n_slots = lb_f.shape[0]
    assert t % HG_TILE == 0 and lc % CHUNK == 0
    lat = lambda g: pl.BlockSpec((1, 1, t, HEAD), lambda b, h: (b, g * nh + h, 0, 0))
    ctx = lambda g: pl.BlockSpec((1, 1, lc, HEAD), lambda b, h: (g * nh + h, b, 0, 0))
    lbs = pl.BlockSpec((n_slots, HEAD), lambda b, h: (0, h))
    staged = pltpu.VMEM((t, HEAD), BF16)
    return pl.pallas_call(
        functools.partial(_hg_kernel, t=t),
        out_shape=jax.ShapeDtypeStruct((bsz, nh, t, HEAD), BF16),
        grid=(bsz, nh),
        in_specs=[lat(4), lat(5), lat(6), lat(7), lat(8), ctx(2), ctx(3), ctx(4), lbs, lbs,
                  pl.BlockSpec((1, HEAD), lambda b, h: (0, 0))],
        out_specs=pl.BlockSpec((1, 1, t, HEAD), lambda b, h: (b, h, 0, 0)),
        scratch_shapes=[pltpu.VMEM((t, HEAD), F32), staged, staged, staged, staged,
                        pltpu.VMEM((t // HG_TILE, HG_CHUNKS, HEAD), F32), pltpu.VMEM((t, HEAD), F32)],
        compiler_params=_params(("arbitrary", "arbitrary")),
        name="hgrn2",
    )(proj, proj, proj, proj, proj, proj_ctx, proj_ctx, proj_ctx, lb_f, lb_b, norm_w)


def _merge_kernel(ya_ref, yb_ref, wa_ref, wb_ref, ga_ref, gb_ref, o_ref, ya_s, yb_s):
    @pl.when(pl.program_id(2) == 0)
    def _():
        for h in range(ya_ref.shape[1]):
            ya_s[:, h * HEAD:(h + 1) * HEAD] = ya_ref[0, h]
            yb_s[:, h * HEAD:(h + 1) * HEAD] = yb_ref[0, h]

    ta = jnp.dot(ya_s[...], wa_ref[...], preferred_element_type=F32)
    tb = jnp.dot(yb_s[...], wb_ref[...], preferred_element_type=F32)
    for s in range(ga_ref.shape[1]):
        cs = slice(s * HEAD, (s + 1) * HEAD)
        m = (jax.nn.sigmoid(ga_ref[0, s].astype(F32)) * ta[:, cs]
             + jax.nn.sigmoid(gb_ref[0, s].astype(F32)) * tb[:, cs])
        o_ref[0, :, cs] = m.astype(BF16)


def _merge(ya, yb, w_pa, w_pb, proj, nh):
    bsz, _, t, _ = ya.shape
    d = w_pa.shape[1]
    tm = _tile(t, 512)
    tn = _tile(nh * HEAD, 1024)
    sub = tn // HEAD
    ga0 = 9 * nh // sub
    gb0 = 11 * nh // sub
    assert (9 * nh) % sub == 0 and (11 * nh) % sub == 0
    yspec = pl.BlockSpec((1, nh, tm, HEAD), lambda b, i, j: (b, 0, i, 0))
    wspec = pl.BlockSpec((nh * HEAD, tn), lambda b, i, j: (0, j))
    return pl.pallas_call(
        _merge_kernel,
        out_shape=jax.ShapeDtypeStruct((bsz, t, d), BF16),
        grid=(bsz, t // tm, d // tn),
        in_specs=[yspec, yspec, wspec, wspec,
                  pl.BlockSpec((1, sub, tm, HEAD), lambda b, i, j: (b, ga0 + j, i, 0)),
                  pl.BlockSpec((1, sub, tm, HEAD), lambda b, i, j: (b, gb0 + j, i, 0))],
        out_specs=pl.BlockSpec((1, tm, tn), lambda b, i, j: (b, i, j)),
        scratch_shapes=[pltpu.VMEM((tm, nh * HEAD), BF16), pltpu.VMEM((tm, nh * HEAD), BF16)],
        compiler_params=_params(("arbitrary", "arbitrary", "arbitrary")),
        name="merge",
    )(ya, yb, w_pa, w_pb, proj, proj)


def _out_kernel(m_ref, w_ref, x_ref, gate_ref, lng_ref, lnb_ref, o_ref, *, alpha, tn):
    j = pl.program_id(2)
    out = jnp.dot(m_ref[0], w_ref[...], preferred_element_type=F32)
    col = pl.multiple_of(j * tn, tn)
    o_ref[0, :, pl.ds(col, tn)] = alpha * x_ref[0] + gate_ref[0] * out

    @pl.when(j == pl.num_programs(2) - 1)
    def _():
        def norm_rows(i, carry):
            rs = pl.ds(pl.multiple_of(i * LN_ROWS, LN_ROWS), LN_ROWS)
            r = o_ref[0, rs, :]
            mu = jnp.mean(r, axis=-1, keepdims=True)
            rc = r - mu
            var = jnp.mean(rc * rc, axis=-1, keepdims=True)
            o_ref[0, rs, :] = rc * lax.rsqrt(var + LN_EPS) * lng_ref[...] + lnb_ref[...]
            return carry

        lax.fori_loop(0, o_ref.shape[1] // LN_ROWS, norm_rows, 0)


def _out(m, w_out, x, gate, ln_g, ln_b, alpha):
    bsz, t, d = x.shape
    tm = _tile(t, 512)
    tn = _tile(d, 1024)
    return pl.pallas_call(
        functools.partial(_out_kernel, alpha=alpha, tn=tn),
        out_shape=jax.ShapeDtypeStruct((bsz, t, d), F32),
        grid=(bsz, t // tm, d // tn),
        in_specs=[pl.BlockSpec((1, tm, d), lambda b, i, j: (b, i, 0)),
                  pl.BlockSpec((d, tn), lambda b, i, j: (0, j)),
                  pl.BlockSpec((1, tm, tn), lambda b, i, j: (b, i, j)),
                  pl.BlockSpec((1, 1, tn), lambda b, i, j: (b, 0, j)),
                  pl.BlockSpec((1, d), lambda b, i, j: (0, 0)),
                  pl.BlockSpec((1, d), lambda b, i, j: (0, 0))],
        out_specs=pl.BlockSpec((1, tm, d), lambda b, i, j: (b, i, 0)),
        compiler_params=_params(("arbitrary", "arbitrary", "arbitrary")),
        name="out",
    )(m, w_out, x, gate, ln_g, ln_b)


def _layer(x, ctx, cond, w_ada, b_ada, w_in, rpb, lb_f, lb_b, norm_w, w_pa, w_pb, w_out, ln_g, ln_b, alpha):
    bsz, t, d = x.shape
    nh = (d // 2) // HEAD
    grp = d // 2
    tn = _tile(grp, 1024)
    per = grp // tn

    mod = _adaln(cond, w_ada, b_ada.reshape(1, -1))
    shift, scale, gate = (mod[:3, k * d:(k + 1) * d].reshape(3, 1, d) for k in range(3))
    ctx_row = bsz

    h_lat = _ln_mod(x, shift, scale, lambda b: b)
    h_ctx = _ln_mod(ctx, shift, scale, lambda b: ctx_row)

    first, w_in16 = _proj_first(h_lat, w_in)
    proj, w_pa16, w_pb16, w_out16 = _proj_main(h_lat, w_in16, first, tn, (w_pa, w_pb, w_out))
    proj_ctx = _proj(h_ctx.reshape(1, -1, d), w_in16, 5 * per,
                     lambda j: jnp.where(j < 2 * per, j + per, j + 3 * per), tn, "proj_ctx")
    proj_ctx = proj_ctx.reshape(5 * nh, bsz, ctx.shape[1], HEAD)

    y_a = _na(proj, proj_ctx, rpb, nh)
    y_b = _hgrn2(proj, proj_ctx, lb_f, lb_b, norm_w.reshape(1, HEAD), nh)
    m = _merge(y_a, y_b, w_pa16, w_pb16, proj, nh)
    return _out(m, w_out16, x, gate, ln_g.reshape(1, d), ln_b.reshape(1, d), alpha)


def kernel(x, c, ctx, c_ctx, w_ada, b_ada, w_in, na_rpb, hg_lb_fwd, hg_lb_bwd, hg_norm_w, w_pa, w_pb, w_out,
           ln_g, ln_b):
    depth = w_ada.shape[0]
    assert depth == 1, "the context stream update of deeper stacks is not implemented"
    bsz, _, d = x.shape
    alpha = (2.0 * depth) ** 0.25
    cond = jnp.concatenate([c, c_ctx[None], jnp.zeros((8 - bsz - 1, d), c.dtype)], axis=0)
    return _layer(x, ctx, cond, w_ada[0], b_ada[0], w_in[0], na_rpb[0], hg_lb_fwd, hg_lb_bwd, hg_norm_w[0],
                  w_pa[0], w_pb[0], w_out[0], ln_g[0], ln_b[0], alpha)
```

```python
import functools
import math

import jax
import jax.numpy as jnp
from jax import lax
from jax.experimental import pallas as pl
from jax.experimental.pallas import tpu as pltpu

F32 = jnp.float32
BF16 = jnp.bfloat16

GRID_W = 64
HEAD = 128
NA_KH = 8
NA_KW = 16
NA_QROWS = 4
NA_KROWS = 12
NA_UNROLL = 4
CHUNK = 64
HG_CHUNKS = 16
HG_TILE = HG_CHUNKS * CHUNK
LN_ROWS = 64
LOG2E = math.log2(math.e)
LN_EPS = 1e-6
RMS_EPS = 1e-6
VMEM_LIMIT = 56 * 1024 * 1024


def _params(semantics, vmem=VMEM_LIMIT):
    return pltpu.CompilerParams(dimension_semantics=semantics, vmem_limit_bytes=vmem)


def _tile(n, pref):
    t = min(n, pref)
    assert n % t == 0, (n, t)
    return t


def _adaln_kernel(c_ref, w_ref, b_ref, o_ref):
    c = c_ref[...]
    s = c * jax.nn.sigmoid(c)
    s_hi = s.astype(BF16)
    s_lo = (s - s_hi.astype(F32)).astype(BF16)
    w = w_ref[...]
    w_hi = w.astype(BF16)
    w_lo = (w - w_hi.astype(F32)).astype(BF16)
    n = s.shape[0]
    lhs = jnp.concatenate([s_hi, s_lo], axis=0)
    a = jnp.dot(lhs, w_hi, preferred_element_type=F32)
    b = jnp.dot(s_hi, w_lo, preferred_element_type=F32)
    o_ref[...] = a[:n] + a[n:] + b + b_ref[...]


def _adaln(cond, w, b):
    d, n = w.shape
    tn = _tile(n, 512)
    return pl.pallas_call(
        _adaln_kernel,
        out_shape=jax.ShapeDtypeStruct((cond.shape[0], n), F32),
        grid=(n // tn,),
        in_specs=[pl.BlockSpec(cond.shape, lambda j: (0, 0)),
                  pl.BlockSpec((d, tn), lambda j: (0, j)),
                  pl.BlockSpec((1, tn), lambda j: (0, j))],
        out_specs=pl.BlockSpec((cond.shape[0], tn), lambda j: (0, j)),
        compiler_params=_params(("arbitrary",)),
        name="adaln",
    )(cond, w, b)


def _ln_mod_kernel(x_ref, shift_ref, scale_ref, o_ref):
    x = x_ref[0]
    mu = jnp.mean(x, axis=-1, keepdims=True)
    xc = x - mu
    var = jnp.mean(xc * xc, axis=-1, keepdims=True)
    y = xc * lax.rsqrt(var + LN_EPS)
    o_ref[0] = (y * (1.0 + scale_ref[0]) + shift_ref[0]).astype(BF16)


def _ln_mod(x, shift, scale, cond_row):
    bsz, t, d = x.shape
    tm = _tile(t, 512)
    cmap = lambda b, i: (cond_row(b), 0, 0)
    return pl.pallas_call(
        _ln_mod_kernel,
        out_shape=jax.ShapeDtypeStruct((bsz, t, d), BF16),
        grid=(bsz, t // tm),
        in_specs=[pl.BlockSpec((1, tm, d), lambda b, i: (b, i, 0)),
                  pl.BlockSpec((1, 1, d), cmap),
                  pl.BlockSpec((1, 1, d), cmap)],
        out_specs=pl.BlockSpec((1, tm, d), lambda b, i: (b, i, 0)),
        compiler_params=_params(("arbitrary", "arbitrary")),
        name="ln_mod",
    )(x, shift, scale)


def _store_head_major(o_ref, acc):
    for s in range(o_ref.shape[1]):
        o_ref[0, s] = acc[:, s * HEAD:(s + 1) * HEAD].astype(BF16)


def _proj_kernel(h_ref, w_ref, o_ref):
    _store_head_major(o_ref, jnp.dot(h_ref[0], w_ref[...], preferred_element_type=F32))


def _proj_cast_kernel(h_ref, w_ref, o_ref, w16_ref):
    w16 = w_ref[...].astype(BF16)
    w16_ref[...] = w16
    _store_head_major(o_ref, jnp.dot(h_ref[0], w16, preferred_element_type=F32))


def _proj_first(h, w):
    _, t, d = h.shape
    n = w.shape[1]
    tm = _tile(t, 1024)
    tn = next(c for c in (512, 256, HEAD) if n % c == 0)
    sub = tn // HEAD
    return pl.pallas_call(
        _proj_cast_kernel,
        out_shape=(jax.ShapeDtypeStruct((1, n // HEAD, tm, HEAD), BF16), jax.ShapeDtypeStruct((d, n), BF16)),
        grid=(n // tn,),
        in_specs=[pl.BlockSpec((1, tm, d), lambda j: (0, 0, 0)),
                  pl.BlockSpec((d, tn), lambda j: (0, j))],
        out_specs=[pl.BlockSpec((1, sub, tm, HEAD), lambda j: (0, j, 0, 0)),
                   pl.BlockSpec((d, tn), lambda j: (0, j))],
        compiler_params=_params(("arbitrary",)),
        name="proj_first",
    )(h, w)


def _proj(h, w, n_tiles, col_tile, tn, name):
    bsz, t, d = h.shape
    tm = _tile(t, 1024)
    sub = tn // HEAD
    return pl.pallas_call(
        _proj_kernel,
        out_shape=jax.ShapeDtypeStruct((bsz, n_tiles * sub, t, HEAD), BF16),
        grid=(bsz, t // tm, n_tiles),
        in_specs=[pl.BlockSpec((1, tm, d), lambda b, i, j: (b, i, 0)),
                  pl.BlockSpec((d, tn), lambda b, i, j: (0, col_tile(j)))],
        out_specs=pl.BlockSpec((1, sub, tm, HEAD), lambda b, i, j: (b, j, i, 0)),
        compiler_params=_params(("arbitrary", "arbitrary", "arbitrary")),
        name=name,
    )(h, w)


def _proj_main_kernel(h_ref, w_ref, first_ref, *refs, n_cast, tm):
    o_ref = refs[n_cast]
    rp = pl.program_id(1)

    @pl.when(rp == 0)
    def _():
        o_ref[0, :, 0:tm, :] = first_ref[0]

    acc = jnp.dot(h_ref[0], w_ref[...], preferred_element_type=F32)
    rows = pl.ds(pl.multiple_of(((rp + 1) % 2) * tm, tm), tm)
    for s in range(o_ref.shape[1]):
        o_ref[0, s, rows, :] = acc[:, s * HEAD:(s + 1) * HEAD].astype(BF16)
    for src, dst in zip(refs[:n_cast], refs[n_cast + 1:]):
        dst[...] = src[...].astype(BF16)


def _proj_main(h, w, first, tn, cast):
    bsz, t, d = h.shape
    n = w.shape[1]
    tm = _tile(t, 1024)
    nb = t // tm
    sub = tn // HEAD
    n_tiles = n // tn
    n_rows = bsz * nb - 1
    assert nb % 2 == 0 and first.shape == (1, n // HEAD, tm, HEAD)
    blk = lambda rp: ((rp + 1) // nb, (rp + 1) % nb)
    in_specs = [pl.BlockSpec((1, tm, d), lambda j, rp: (*blk(rp), 0)),
                pl.BlockSpec((d, tn), lambda j, rp: (0, j)),
                pl.BlockSpec((1, sub, tm, HEAD), lambda j, rp: (0, j, 0, 0))]
    out_specs = [pl.BlockSpec((1, sub, 2 * tm, HEAD), lambda j, rp: (blk(rp)[0], j, blk(rp)[1] // 2, 0))]
    out_shape = [jax.ShapeDtypeStruct((bsz, n // HEAD, t, HEAD), BF16)]
    for cw in cast:
        rows, cols = cw.shape
        split = next(s for s in range(n_rows * n_tiles, 0, -1) if rows % s == 0 and (rows // s) % 16 == 0)
        spec = pl.BlockSpec((rows // split, cols),
                            lambda j, rp, split=split: (jnp.minimum(j * n_rows + rp, split - 1), 0))
        in_specs.append(spec)
        out_specs.append(spec)
        out_shape.append(jax.ShapeDtypeStruct(cw.shape, BF16))
    return pl.pallas_call(
        functools.partial(_proj_main_kernel, n_cast=len(cast), tm=tm),
        out_shape=out_shape,
        grid=(n_tiles, n_rows),
        in_specs=in_specs,
        out_specs=out_specs,
        compiler_params=_params(("arbitrary", "arbitrary")),
        name="proj_lat",
    )(h, w, first, *cast)


N_DROW = 2 * NA_KH - 1
N_DCOL = 2 * NA_KW - 1


def _na_block_kinds(rows):
    return ((0, 0), (NA_QROWS, 0), (rows - NA_QROWS, rows - NA_KROWS))


def _build_bias(rpb_ref, head, band_s, bias_s, rows):
    shape = (GRID_W, 2 * GRID_W)
    qc = lax.broadcasted_iota(jnp.int32, shape, 0)
    lane = lax.broadcasted_iota(jnp.int32, shape, 1)
    kc = lane & (GRID_W - 1)
    d_col = jnp.clip(kc - qc, -(NA_KW - 1), NA_KW - 1) + (NA_KW - 1)
    col_start = jnp.clip(qc - NA_KW // 2, 0, GRID_W - NA_KW)
    col_off = kc - col_start
    neg = jnp.full(shape, -jnp.inf, F32)
    base = head * (N_DROW * N_DCOL)
    for d in range(N_DROW):
        acc = neg
        for j in range(N_DCOL):
            acc = jnp.where(d_col == j, rpb_ref[base + d * N_DCOL + j] * LOG2E, acc)
        band_s[d] = jnp.where(col_off < 0, neg, jnp.where(col_off < NA_KW, acc, neg))
    left = lane < GRID_W
    for kind, (r0, kb) in enumerate(_na_block_kinds(rows)):
        for qi in range(NA_QROWS):
            r = r0 + qi
            row_start = min(max(r - NA_KH // 2, 0), rows - NA_KH)

            def band(kr):
                return band_s[kr - r + NA_KH - 1] if row_start <= kr < row_start + NA_KH else neg

            for jp in range(NA_KROWS // 2):
                kr = kb + 2 * jp
                bias_s[kind, qi * GRID_W:(qi + 1) * GRID_W, jp * 2 * GRID_W:(jp + 1) * 2 * GRID_W] = (
                    jnp.where(left, band(kr), band(kr + 1)))


def _na_kernel(rpb_ref, q_ref, k_ref, v_ref, z_ref, kc_ref, vc_ref, o_ref, band_s, bias_s, *, rows):
    nq = NA_QROWS * GRID_W
    nk = NA_KROWS * GRID_W
    n_blocks = rows // NA_QROWS
    scale = HEAD ** -0.5 * LOG2E
    kc = kc_ref[0, 0]
    vc = vc_ref[0, 0]
    nt = (((1,), (1,)), ((), ()))

    @pl.when(pl.program_id(1) == 0)
    def _():
        _build_bias(rpb_ref, pl.program_id(0), band_s, bias_s, rows)

    def scores(i):
        r0 = i * NA_QROWS
        kb = jnp.clip(r0 - NA_KH // 2, 0, rows - NA_KROWS)
        kind = jnp.where(i == 0, 0, jnp.where(i == n_blocks - 1, 2, 1))
        q_off = pl.multiple_of(r0 * GRID_W, GRID_W)
        k_off = pl.multiple_of(kb * GRID_W, GRID_W)
        q = (q_ref[0, 0, pl.ds(q_off, nq), :].astype(F32) * scale).astype(BF16)
        s_loc = lax.dot_general(q, k_ref[0, 0, pl.ds(k_off, nk), :], nt, preferred_element_type=F32)
        s_ctx = lax.dot_general(q, kc, nt, preferred_element_type=F32)
        return q_off, k_off, s_loc + bias_s[kind], s_ctx

    def attend(q_off, k_off, s_loc, s_ctx):
        m = jnp.maximum(jnp.max(s_loc, axis=-1, keepdims=True), jnp.max(s_ctx, axis=-1, keepdims=True))
        p_loc = jnp.exp2(s_loc - m)
        p_ctx = jnp.exp2(s_ctx - m)
        denom = jnp.sum(p_loc, axis=-1, keepdims=True) + jnp.sum(p_ctx, axis=-1, keepdims=True)
        o = (jnp.dot(p_loc.astype(BF16), v_ref[0, 0, pl.ds(k_off, nk), :], preferred_element_type=F32)
             + jnp.dot(p_ctx.astype(BF16), vc, preferred_element_type=F32))
        o = o / denom
        z = z_ref[0, 0, pl.ds(q_off, nq), :].astype(F32)
        o_ref[0, pl.ds(q_off, nq), :] = (o * (z * jax.nn.sigmoid(z))).astype(BF16)

    def blocks(n, carry):
        staged = [scores(n * NA_UNROLL + u) for u in range(NA_UNROLL)]
        for args in staged:
            attend(*args)
        return carry

    lax.fori_loop(0, n_blocks // NA_UNROLL, blocks, 0)


def _na(proj, proj_ctx, rpb, nh):
    bsz, _, t, _ = proj.shape
    lc = proj_ctx.shape[2]
    rows = t // GRID_W
    assert rows % (NA_QROWS * NA_UNROLL) == 0 and rows >= NA_KROWS + NA_QROWS
    assert rpb.shape == (nh, N_DROW, N_DCOL)
    lat = lambda g: pl.BlockSpec((1, 1, t, HEAD), lambda h, b: (b, g * nh + h, 0, 0))
    ctx = lambda g: pl.BlockSpec((1, 1, lc, HEAD), lambda h, b: (g * nh + h, b, 0, 0))
    return pl.pallas_call(
        functools.partial(_na_kernel, rows=rows),
        out_shape=jax.ShapeDtypeStruct((bsz, t, nh * HEAD), BF16),
        grid=(nh, bsz),
        in_specs=[pl.BlockSpec(memory_space=pltpu.SMEM),
                  lat(0), lat(1), lat(2), lat(3), ctx(0), ctx(1)],
        out_specs=pl.BlockSpec((1, t, HEAD), lambda h, b: (b, 0, h)),
        scratch_shapes=[pltpu.VMEM((N_DROW, GRID_W, 2 * GRID_W), F32),
                        pltpu.VMEM((3, NA_QROWS * GRID_W, NA_KROWS * GRID_W), F32)],
        compiler_params=_params(("arbitrary", "arbitrary")),
        name="na",
    )(rpb.astype(F32).reshape(-1), proj, proj, proj, proj, proj_ctx, proj_ctx)


def _seg_cumsum(x, reverse):
    r_i = lax.broadcasted_iota(jnp.int32, (CHUNK, 2 * CHUNK), 0)
    c_i = lax.broadcasted_iota(jnp.int32, (CHUNK, 2 * CHUNK), 1) & (CHUNK - 1)
    tri2 = jnp.where((r_i <= c_i) if reverse else (r_i >= c_i), 1.0, 0.0).astype(BF16)
    hi = x.astype(BF16)
    lo = (x - hi.astype(F32)).astype(BF16)
    out = []
    for c in range(x.shape[0] // CHUNK):
        rows = slice(c * CHUNK, (c + 1) * CHUNK)
        out.append(jnp.dot(tri2, jnp.concatenate([hi[rows], lo[rows]], axis=0), preferred_element_type=F32))
    return jnp.concatenate(out, axis=0)


def _forget(f_pre, lb):
    sg = jax.nn.sigmoid(f_pre)
    one_m_lb = 1.0 - lb
    return one_m_lb * (1.0 - sg), jnp.log(lb + one_m_lb * sg)


def _lower_bound(logits):
    m = jnp.max(logits, axis=0, keepdims=True)
    e = jnp.exp(logits - m)
    return e[0:1] / jnp.sum(e, axis=0, keepdims=True)


def _gates(f_pre, lb, reverse):
    kk, logf = _forget(f_pre, lb)
    cum = _seg_cumsum(logf, reverse).reshape(-1, CHUNK, HEAD)
    mid = CHUNK // 2
    ref = cum[:, mid:mid + 1] if reverse else cum[:, mid - 1:mid]
    last = cum[:, 0:1] if reverse else cum[:, CHUNK - 1:CHUNK]
    return kk.reshape(cum.shape), cum, ref, last


def _decay_columns(dec):
    pad = jnp.zeros((HEAD - dec.shape[0], HEAD), F32)
    return jnp.concatenate([dec, pad], axis=0).T


def _ctx_state(f_pre, v, lb, reverse):
    kk, cum, _, last = _gates(f_pre, lb, reverse)
    kl = (kk * jnp.exp(last - cum)).astype(BF16)
    dec_t = _decay_columns(jnp.exp(last).reshape(-1, HEAD))
    tn = (((0,), (0,)), ((), ()))
    n_chunks = kl.shape[0]
    st = jnp.zeros((HEAD, HEAD), F32)
    for c in (reversed(range(n_chunks)) if reverse else range(n_chunks)):
        u = lax.dot_general(kl[c], v[c * CHUNK:(c + 1) * CHUNK], tn, preferred_element_type=F32)
        st = st * dec_t[:, c:c + 1] + u
    return st


def _hg_prepare(qs, f_pre, lb, reverse):
    kk, cum, ref, last = _gates(f_pre, lb, reverse)
    e = jnp.exp(cum - ref)
    qd = qs.reshape(cum.shape) * e
    kd = kk / e
    qe = qd * jnp.exp(ref)
    kl = kd * jnp.exp(last - ref)
    flat = lambda a: a.reshape(-1, HEAD).astype(BF16)
    return flat(qd), flat(kd), flat(qe), flat(kl), jnp.exp(last).reshape(-1, HEAD)


def _hg_kernel(q_ref, ff_ref, fb_ref, i_ref, g_ref, ffc_ref, fbc_ref, ic_ref, lbf_ref, lbb_ref, nw_ref,
               o_ref, qs_s, qd_s, kd_s, qe_s, kl_s, dec_s, ob_s, *, t):
    n_tiles = t // HG_TILE
    lb_f = _lower_bound(lbf_ref[...])
    lb_b = _lower_bound(lbb_ref[...])
    nw = nw_ref[...]
    r_i = lax.broadcasted_iota(jnp.int32, (CHUNK, CHUNK), 0)
    c_i = lax.broadcasted_iota(jnp.int32, (CHUNK, CHUNK), 1)
    nt = (((1,), (1,)), ((), ()))
    tn = (((0,), (0,)), ((), ()))

    def prepare(tix, f_ref, lb, reverse, first):
        rs = pl.ds(pl.multiple_of(tix * HG_TILE, HG_TILE), HG_TILE)
        if first:
            q = q_ref[0, 0, rs, :].astype(F32)
            qs = q * jax.nn.sigmoid(q)
        else:
            qs = qs_s[rs, :]
        return (qs if first else None,) + _hg_prepare(qs, f_ref[0, 0, rs, :].astype(F32), lb, reverse)

    def stage(tix, vals):
        rs = pl.ds(pl.multiple_of(tix * HG_TILE, HG_TILE), HG_TILE)
        qs, qd, kd, qe, kl, dec = vals
        if qs is not None:
            qs_s[rs, :] = qs
        qd_s[rs, :] = qd
        kd_s[rs, :] = kd
        qe_s[rs, :] = qe
        kl_s[rs, :] = kl
        dec_s[tix] = dec

    def scan(tix, st, reverse, emit):
        tri = (r_i <= c_i) if reverse else (r_i >= c_i)
        order = [HG_CHUNKS - 1 - u if reverse else u for u in range(HG_CHUNKS)]
        cs = {j: pl.ds(pl.multiple_of(tix * HG_TILE + j * CHUNK, CHUNK), CHUNK) for j in order}
        v = {j: i_ref[0, 0, cs[j], :] for j in order}
        a = {j: lax.dot_general(qd_s[cs[j], :], kd_s[cs[j], :], nt, preferred_element_type=F32) for j in order}
        u = {j: lax.dot_general(kl_s[cs[j], :], v[j], tn, preferred_element_type=F32) for j in order}
        dec_t = _decay_columns(dec_s[tix])
        dec_b = {j: jnp.broadcast_to(dec_t[:, j:j + 1], (HEAD, HEAD)) for j in order}
        for j in order:
            lhs = jnp.concatenate([qe_s[cs[j], :], jnp.where(tri, a[j], 0.0).astype(BF16)], axis=1)
            rhs = jnp.concatenate([st.astype(BF16), v[j]], axis=0)
            emit(cs[j], jnp.dot(lhs, rhs, preferred_element_type=F32))
            st = st * dec_b[j] + u[j]
        return st

    def direction(f_ref, fc_ref, lb, reverse, first, emit):
        tile_of = (lambda n: n_tiles - 1 - n) if reverse else (lambda n: n)
        stage(tile_of(0), prepare(tile_of(0), f_ref, lb, reverse, first))
        st = _ctx_state(fc_ref[0, 0].astype(F32), ic_ref[0, 0], lb, reverse)

        def trip(n, st):
            nxt = tile_of(jnp.minimum(n + 1, n_tiles - 1))
            vals = prepare(nxt, f_ref, lb, reverse, first)
            st = scan(tile_of(n), st, reverse, emit)
            stage(nxt, vals)
            return st

        lax.fori_loop(0, n_tiles, trip, st)

    def emit_backward(cs, o):
        ob_s[cs, :] = o

    def emit_forward(cs, o):
        o = o + ob_s[cs, :]
        o = o * lax.rsqrt(jnp.mean(o * o, axis=-1, keepdims=True) + RMS_EPS) * nw
        g = g_ref[0, 0, cs, :].astype(F32)
        o_ref[0, cs, :] = (o * (g * jax.nn.sigmoid(g))).astype(BF16)

    direction(fb_ref, fbc_ref, lb_b, True, True, emit_backward)
    direction(ff_ref, ffc_ref, lb_f, False, False, emit_forward)


def _hgrn2(proj, proj_ctx, lb_f, lb_b, norm_w, nh):
    bsz, _, t, _ = proj.shape
    lc = proj_ctx.shape[2]
    n_slots = lb_f.shape[0]
    assert t % HG_TILE == 0 and lc % CHUNK == 0
    lat = lambda g: pl.BlockSpec((1, 1, t, HEAD), lambda b, h: (b, g * nh + h, 0, 0))
    ctx = lambda g: pl.BlockSpec((1, 1, lc, HEAD), lambda b, h: (g * nh + h, b, 0, 0))
    lbs = pl.BlockSpec((n_slots, HEAD), lambda b, h: (0, h))
    staged = pltpu.VMEM((t, HEAD), BF16)
    return pl.pallas_call(
        functools.partial(_hg_kernel, t=t),
        out_shape=jax.ShapeDtypeStruct((bsz, t, nh * HEAD), BF16),
        grid=(bsz, nh),
        in_specs=[lat(4), lat(5), lat(6), lat(7), lat(8), ctx(2), ctx(3), ctx(4), lbs, lbs,
                  pl.BlockSpec((1, HEAD), lambda b, h: (0, 0))],
        out_specs=pl.BlockSpec((1, t, HEAD), lambda b, h: (b, 0, h)),
        scratch_shapes=[pltpu.VMEM((t, HEAD), F32), staged, staged, staged, staged,
                        pltpu.VMEM((t // HG_TILE, HG_CHUNKS, HEAD), F32), pltpu.VMEM((t, HEAD), F32)],
        compiler_params=_params(("arbitrary", "arbitrary")),
        name="hgrn2",
    )(proj, proj, proj, proj, proj, proj_ctx, proj_ctx, proj_ctx, lb_f, lb_b, norm_w)


def _merge_kernel(ya_ref, yb_ref, wa_ref, wb_ref, ga_ref, gb_ref, o_ref):
    ta = jnp.dot(ya_ref[0], wa_ref[...], preferred_element_type=F32)
    tb = jnp.dot(yb_ref[0], wb_ref[...], preferred_element_type=F32)
    for s in range(ga_ref.shape[1]):
        cs = slice(s * HEAD, (s + 1) * HEAD)
        m = (jax.nn.sigmoid(ga_ref[0, s].astype(F32)) * ta[:, cs]
             + jax.nn.sigmoid(gb_ref[0, s].astype(F32)) * tb[:, cs])
        o_ref[0, :, cs] = m.astype(BF16)


def _merge(ya, yb, w_pa, w_pb, proj, nh):
    bsz, t, _ = ya.shape
    d = w_pa.shape[1]
    tm = _tile(t, 1024)
    tn = _tile(nh * HEAD, 1024)
    sub = tn // HEAD
    ga0 = 9 * nh // sub
    gb0 = 11 * nh // sub
    assert (9 * nh) % sub == 0 and (11 * nh) % sub == 0
    yspec = pl.BlockSpec((1, tm, nh * HEAD), lambda b, i, j: (b, i, 0))
    wspec = pl.BlockSpec((nh * HEAD, tn), lambda b, i, j: (0, j))
    return pl.pallas_call(
        _merge_kernel,
        out_shape=jax.ShapeDtypeStruct((bsz, t, d), BF16),
        grid=(bsz, t // tm, d // tn),
        in_specs=[yspec, yspec, wspec, wspec,
                  pl.BlockSpec((1, sub, tm, HEAD), lambda b, i, j: (b, ga0 + j, i, 0)),
                  pl.BlockSpec((1, sub, tm, HEAD), lambda b, i, j: (b, gb0 + j, i, 0))],
        out_specs=pl.BlockSpec((1, tm, tn), lambda b, i, j: (b, i, j)),
        compiler_params=_params(("arbitrary", "arbitrary", "arbitrary")),
        name="merge",
    )(ya, yb, w_pa, w_pb, proj, proj)


def _out_kernel(m_ref, w_ref, x_ref, gate_ref, lng_ref, lnb_ref, o_ref, *, alpha, tn):
    j = pl.program_id(2)
    out = jnp.dot(m_ref[0], w_ref[...], preferred_element_type=F32)
    col = pl.multiple_of(j * tn, tn)
    o_ref[0, :, pl.ds(col, tn)] = alpha * x_ref[0] + gate_ref[0] * out

    @pl.when(j == pl.num_programs(2) - 1)
    def _():
        def norm_rows(i, carry):
            rs = pl.ds(pl.multiple_of(i * LN_ROWS, LN_ROWS), LN_ROWS)
            r = o_ref[0, rs, :]
            mu = jnp.mean(r, axis=-1, keepdims=True)
            rc = r - mu
            var = jnp.mean(rc * rc, axis=-1, keepdims=True)
            o_ref[0, rs, :] = rc * lax.rsqrt(var + LN_EPS) * lng_ref[...] + lnb_ref[...]
            return carry

        lax.fori_loop(0, o_ref.shape[1] // LN_ROWS, norm_rows, 0)


def _out(m, w_out, x, gate, ln_g, ln_b, alpha):
    bsz, t, d = x.shape
    tm = _tile(t, 512)
    tn = _tile(d, 1024)
    return pl.pallas_call(
        functools.partial(_out_kernel, alpha=alpha, tn=tn),
        out_shape=jax.ShapeDtypeStruct((bsz, t, d), F32),
        grid=(bsz, t // tm, d // tn),
        in_specs=[pl.BlockSpec((1, tm, d), lambda b, i, j: (b, i, 0)),
                  pl.BlockSpec((d, tn), lambda b, i, j: (0, j)),
                  pl.BlockSpec((1, tm, tn), lambda b, i, j: (b, i, j)),
                  pl.BlockSpec((1, 1, tn), lambda b, i, j: (b, 0, j)),
                  pl.BlockSpec((1, d), lambda b, i, j: (0, 0)),
                  pl.BlockSpec((1, d), lambda b, i, j: (0, 0))],
        out_specs=pl.BlockSpec((1, tm, d), lambda b, i, j: (b, i, 0)),
        compiler_params=_params(("arbitrary", "arbitrary", "arbitrary")),
        name="out",
    )(m, w_out, x, gate, ln_g, ln_b)


def _layer(x, ctx, cond, w_ada, b_ada, w_in, rpb, lb_f, lb_b, norm_w, w_pa, w_pb, w_out, ln_g, ln_b, alpha):
    bsz, t, d = x.shape
    nh = (d // 2) // HEAD
    grp = d // 2
    tn = _tile(grp, 1024)
    per = grp // tn

    mod = _adaln(cond, w_ada, b_ada.reshape(1, -1))
    shift, scale, gate = (mod[:3, k * d:(k + 1) * d].reshape(3, 1, d) for k in range(3))
    ctx_row = bsz

    h_lat = _ln_mod(x, shift, scale, lambda b: b)
    h_ctx = _ln_mod(ctx, shift, scale, lambda b: ctx_row)

    first, w_in16 = _proj_first(h_lat, w_in)
    proj, w_pa16, w_pb16, w_out16 = _proj_main(h_lat, w_in16, first, tn, (w_pa, w_pb, w_out))
    proj_ctx = _proj(h_ctx.reshape(1, -1, d), w_in16, 5 * per,
                     lambda j: jnp.where(j < 2 * per, j + per, j + 3 * per), tn, "proj_ctx")
    proj_ctx = proj_ctx.reshape(5 * nh, bsz, ctx.shape[1], HEAD)

    y_a = _na(proj, proj_ctx, rpb, nh)
    y_b = _hgrn2(proj, proj_ctx, lb_f, lb_b, norm_w.reshape(1, HEAD), nh)
    m = _merge(y_a, y_b, w_pa16, w_pb16, proj, nh)
    return _out(m, w_out16, x, gate, ln_g.reshape(1, d), ln_b.reshape(1, d), alpha)


def kernel(x, c, ctx, c_ctx, w_ada, b_ada, w_in, na_rpb, hg_lb_fwd, hg_lb_bwd, hg_norm_w, w_pa, w_pb, w_out,
           ln_g, ln_b):
    depth = w_ada.shape[0]
    assert depth == 1, "the context stream update of deeper stacks is not implemented"
    bsz, _, d = x.shape
    alpha = (2.0 * depth) ** 0.25
    cond = jnp.concatenate([c, c_ctx[None], jnp.zeros((8 - bsz - 1, d), c.dtype)], axis=0)
    return _layer(x, ctx, cond, w_ada[0], b_ada[0], w_in[0], na_rpb[0], hg_lb_fwd, hg_lb_bwd, hg_norm_w[0],
                  w_pa[0], w_pb[0], w_out[0], ln_g[0], ln_b[0], alpha)
```

```python
import functools
import math

import jax
import jax.numpy as jnp
from jax import lax
from jax.experimental import pallas as pl
from jax.experimental.pallas import tpu as pltpu

F32 = jnp.float32
BF16 = jnp.bfloat16

GRID_W = 64
HEAD = 128
NA_KH = 8
NA_KW = 16
NA_QROWS = 4
NA_KROWS = 12
NA_UNROLL = 4
CHUNK = 64
HG_CHUNKS = 16
HG_TILE = HG_CHUNKS * CHUNK
LN_ROWS = 64
LOG2E = math.log2(math.e)
LN_EPS = 1e-6
RMS_EPS = 1e-6
VMEM_LIMIT = 56 * 1024 * 1024


def _params(semantics, vmem=VMEM_LIMIT):
    return pltpu.CompilerParams(dimension_semantics=semantics, vmem_limit_bytes=vmem)


def _tile(n, pref):
    t = min(n, pref)
    assert n % t == 0, (n, t)
    return t


def _adaln_kernel(c_ref, w_ref, b_ref, o_ref):
    c = c_ref[...]
    s = c * jax.nn.sigmoid(c)
    s_hi = s.astype(BF16)
    s_lo = (s - s_hi.astype(F32)).astype(BF16)
    w = w_ref[...]
    w_hi = w.astype(BF16)
    w_lo = (w - w_hi.astype(F32)).astype(BF16)
    n = s.shape[0]
    lhs = jnp.concatenate([s_hi, s_lo], axis=0)
    a = jnp.dot(lhs, w_hi, preferred_element_type=F32)
    b = jnp.dot(s_hi, w_lo, preferred_element_type=F32)
    o_ref[...] = a[:n] + a[n:] + b + b_ref[...]


def _adaln(cond, w, b):
    d, n = w.shape
    tn = _tile(n, 512)
    return pl.pallas_call(
        _adaln_kernel,
        out_shape=jax.ShapeDtypeStruct((cond.shape[0], n), F32),
        grid=(n // tn,),
        in_specs=[pl.BlockSpec(cond.shape, lambda j: (0, 0)),
                  pl.BlockSpec((d, tn), lambda j: (0, j)),
                  pl.BlockSpec((1, tn), lambda j: (0, j))],
        out_specs=pl.BlockSpec((cond.shape[0], tn), lambda j: (0, j)),
        compiler_params=_params(("arbitrary",)),
        name="adaln",
    )(cond, w, b)


def _ln_mod_kernel(x_ref, shift_ref, scale_ref, o_ref):
    x = x_ref[0]
    mu = jnp.mean(x, axis=-1, keepdims=True)
    xc = x - mu
    var = jnp.mean(xc * xc, axis=-1, keepdims=True)
    y = xc * lax.rsqrt(var + LN_EPS)
    o_ref[0] = (y * (1.0 + scale_ref[0]) + shift_ref[0]).astype(BF16)


def _ln_mod(x, shift, scale, cond_row):
    bsz, t, d = x.shape
    tm = _tile(t, 512)
    cmap = lambda b, i: (cond_row(b), 0, 0)
    return pl.pallas_call(
        _ln_mod_kernel,
        out_shape=jax.ShapeDtypeStruct((bsz, t, d), BF16),
        grid=(bsz, t // tm),
        in_specs=[pl.BlockSpec((1, tm, d), lambda b, i: (b, i, 0)),
                  pl.BlockSpec((1, 1, d), cmap),
                  pl.BlockSpec((1, 1, d), cmap)],
        out_specs=pl.BlockSpec((1, tm, d), lambda b, i: (b, i, 0)),
        compiler_params=_params(("arbitrary", "arbitrary")),
        name="ln_mod",
    )(x, shift, scale)


def _store_head_major(o_ref, acc):
    for s in range(o_ref.shape[1]):
        o_ref[0, s] = acc[:, s * HEAD:(s + 1) * HEAD].astype(BF16)


def _proj_kernel(h_ref, w_ref, o_ref):
    _store_head_major(o_ref, jnp.dot(h_ref[0], w_ref[...], preferred_element_type=F32))


def _cast_specs(weights, n_steps, step):
    specs, shapes = [], []
    for w in weights:
        rows, cols = w.shape
        split = next(s for s in range(n_steps, 0, -1) if rows % s == 0 and (rows // s) % 16 == 0)
        specs.append(pl.BlockSpec((rows // split, cols),
                                  lambda *ids, split=split: (jnp.minimum(step(*ids), split - 1), 0)))
        shapes.append(jax.ShapeDtypeStruct(w.shape, BF16))
    return specs, shapes


def _cast_slabs(srcs, dsts):
    for src, dst in zip(srcs, dsts):
        dst[...] = src[...].astype(BF16)


def _proj_cast_kernel(h_ref, w_ref, o_ref, w16_ref):
    w16 = w_ref[...].astype(BF16)
    w16_ref[...] = w16
    _store_head_major(o_ref, jnp.dot(h_ref[0], w16, preferred_element_type=F32))


def _proj_first(h, w):
    _, t, d = h.shape
    n = w.shape[1]
    tm = _tile(t, 1024)
    tn = next(c for c in (512, 256, HEAD) if n % c == 0)
    sub = tn // HEAD
    return pl.pallas_call(
        _proj_cast_kernel,
        out_shape=(jax.ShapeDtypeStruct((1, n // HEAD, tm, HEAD), BF16), jax.ShapeDtypeStruct((d, n), BF16)),
        grid=(n // tn,),
        in_specs=[pl.BlockSpec((1, tm, d), lambda j: (0, 0, 0)),
                  pl.BlockSpec((d, tn), lambda j: (0, j))],
        out_specs=[pl.BlockSpec((1, sub, tm, HEAD), lambda j: (0, j, 0, 0)),
                   pl.BlockSpec((d, tn), lambda j: (0, j))],
        compiler_params=_params(("arbitrary",)),
        name="proj_first",
    )(h, w)


def _proj(h, w, n_tiles, col_tile, tn, name):
    bsz, t, d = h.shape
    tm = _tile(t, 1024)
    sub = tn // HEAD
    return pl.pallas_call(
        _proj_kernel,
        out_shape=jax.ShapeDtypeStruct((bsz, n_tiles * sub, t, HEAD), BF16),
        grid=(bsz, t // tm, n_tiles),
        in_specs=[pl.BlockSpec((1, tm, d), lambda b, i, j: (b, i, 0)),
                  pl.BlockSpec((d, tn), lambda b, i, j: (0, col_tile(j)))],
        out_specs=pl.BlockSpec((1, sub, tm, HEAD), lambda b, i, j: (b, j, i, 0)),
        compiler_params=_params(("arbitrary", "arbitrary", "arbitrary")),
        name=name,
    )(h, w)


def _proj_main_kernel(h_ref, w_ref, first_ref, o_ref, *, tm):
    rp = pl.program_id(1)

    @pl.when(rp == 0)
    def _():
        o_ref[0, :, 0:tm, :] = first_ref[0]

    acc = jnp.dot(h_ref[0], w_ref[...], preferred_element_type=F32)
    rows = pl.ds(pl.multiple_of(((rp + 1) % 2) * tm, tm), tm)
    for s in range(o_ref.shape[1]):
        o_ref[0, s, rows, :] = acc[:, s * HEAD:(s + 1) * HEAD].astype(BF16)


def _proj_main(h, w, first, tn):
    bsz, t, d = h.shape
    n = w.shape[1]
    tm = _tile(t, 1024)
    nb = t // tm
    sub = tn // HEAD
    assert nb % 2 == 0 and first.shape == (1, n // HEAD, tm, HEAD)
    blk = lambda rp: ((rp + 1) // nb, (rp + 1) % nb)
    return pl.pallas_call(
        functools.partial(_proj_main_kernel, tm=tm),
        out_shape=jax.ShapeDtypeStruct((bsz, n // HEAD, t, HEAD), BF16),
        grid=(n // tn, bsz * nb - 1),
        in_specs=[pl.BlockSpec((1, tm, d), lambda j, rp: (*blk(rp), 0)),
                  pl.BlockSpec((d, tn), lambda j, rp: (0, j)),
                  pl.BlockSpec((1, sub, tm, HEAD), lambda j, rp: (0, j, 0, 0))],
        out_specs=pl.BlockSpec((1, sub, 2 * tm, HEAD), lambda j, rp: (blk(rp)[0], j, blk(rp)[1] // 2, 0)),
        compiler_params=_params(("arbitrary", "arbitrary")),
        name="proj_lat",
    )(h, w, first)


N_DROW = 2 * NA_KH - 1
N_DCOL = 2 * NA_KW - 1


def _na_block_kinds(rows):
    return ((0, 0), (NA_QROWS, 0), (rows - NA_QROWS, rows - NA_KROWS))


def _build_bias(rpb_ref, head, band_s, bias_s, rows):
    shape = (GRID_W, 2 * GRID_W)
    qc = lax.broadcasted_iota(jnp.int32, shape, 0)
    lane = lax.broadcasted_iota(jnp.int32, shape, 1)
    kc = lane & (GRID_W - 1)
    d_col = jnp.clip(kc - qc, -(NA_KW - 1), NA_KW - 1) + (NA_KW - 1)
    col_start = jnp.clip(qc - NA_KW // 2, 0, GRID_W - NA_KW)
    col_off = kc - col_start
    neg = jnp.full(shape, -jnp.inf, F32)
    base = head * (N_DROW * N_DCOL)
    for d in range(N_DROW):
        acc = neg
        for j in range(N_DCOL):
            acc = jnp.where(d_col == j, rpb_ref[base + d * N_DCOL + j] * LOG2E, acc)
        band_s[d] = jnp.where(col_off < 0, neg, jnp.where(col_off < NA_KW, acc, neg))
    left = lane < GRID_W
    for kind, (r0, kb) in enumerate(_na_block_kinds(rows)):
        for qi in range(NA_QROWS):
            r = r0 + qi
            row_start = min(max(r - NA_KH // 2, 0), rows - NA_KH)

            def band(kr):
                return band_s[kr - r + NA_KH - 1] if row_start <= kr < row_start + NA_KH else neg

            for jp in range(NA_KROWS // 2):
                kr = kb + 2 * jp
                bias_s[kind, qi * GRID_W:(qi + 1) * GRID_W, jp * 2 * GRID_W:(jp + 1) * 2 * GRID_W] = (
                    jnp.where(left, band(kr), band(kr + 1)))


def _na_kernel(rpb_ref, q_ref, k_ref, v_ref, z_ref, kc_ref, vc_ref, *refs, rows, n_cast):
    o_ref = refs[n_cast]
    band_s, bias_s = refs[2 * n_cast + 1:]
    _cast_slabs(refs[:n_cast], refs[n_cast + 1:2 * n_cast + 1])
    nq = NA_QROWS * GRID_W
    nk = NA_KROWS * GRID_W
    n_blocks = rows // NA_QROWS
    scale = HEAD ** -0.5 * LOG2E
    kc = kc_ref[0, 0]
    vc = vc_ref[0, 0]
    nt = (((1,), (1,)), ((), ()))

    @pl.when(pl.program_id(1) == 0)
    def _():
        _build_bias(rpb_ref, pl.program_id(0), band_s, bias_s, rows)

    def scores(i):
        r0 = i * NA_QROWS
        kb = jnp.clip(r0 - NA_KH // 2, 0, rows - NA_KROWS)
        kind = jnp.where(i == 0, 0, jnp.where(i == n_blocks - 1, 2, 1))
        q_off = pl.multiple_of(r0 * GRID_W, GRID_W)
        k_off = pl.multiple_of(kb * GRID_W, GRID_W)
        q = (q_ref[0, 0, pl.ds(q_off, nq), :].astype(F32) * scale).astype(BF16)
        s_loc = lax.dot_general(q, k_ref[0, 0, pl.ds(k_off, nk), :], nt, preferred_element_type=F32)
        s_ctx = lax.dot_general(q, kc, nt, preferred_element_type=F32)
        return q_off, k_off, s_loc + bias_s[kind], s_ctx

    def attend(q_off, k_off, s_loc, s_ctx):
        m = jnp.maximum(jnp.max(s_loc, axis=-1, keepdims=True), jnp.max(s_ctx, axis=-1, keepdims=True))
        p_loc = jnp.exp2(s_loc - m)
        p_ctx = jnp.exp2(s_ctx - m)
        denom = jnp.sum(p_loc, axis=-1, keepdims=True) + jnp.sum(p_ctx, axis=-1, keepdims=True)
        o = (jnp.dot(p_loc.astype(BF16), v_ref[0, 0, pl.ds(k_off, nk), :], preferred_element_type=F32)
             + jnp.dot(p_ctx.astype(BF16), vc, preferred_element_type=F32))
        o = o / denom
        z = z_ref[0, 0, pl.ds(q_off, nq), :].astype(F32)
        o_ref[0, pl.ds(q_off, nq), :] = (o * (z * jax.nn.sigmoid(z))).astype(BF16)

    def blocks(n, carry):
        staged = [scores(n * NA_UNROLL + u) for u in range(NA_UNROLL)]
        for args in staged:
            attend(*args)
        return carry

    lax.fori_loop(0, n_blocks // NA_UNROLL, blocks, 0)


def _na(proj, proj_ctx, rpb, nh, cast):
    bsz, _, t, _ = proj.shape
    lc = proj_ctx.shape[2]
    rows = t // GRID_W
    assert rows % (NA_QROWS * NA_UNROLL) == 0 and rows >= NA_KROWS + NA_QROWS
    assert rpb.shape == (nh, N_DROW, N_DCOL)
    lat = lambda g: pl.BlockSpec((1, 1, t, HEAD), lambda h, b: (b, g * nh + h, 0, 0))
    ctx = lambda g: pl.BlockSpec((1, 1, lc, HEAD), lambda h, b: (g * nh + h, b, 0, 0))
    cast_specs, cast_shapes = _cast_specs(cast, nh * bsz, lambda h, b: h * bsz + b)
    return pl.pallas_call(
        functools.partial(_na_kernel, rows=rows, n_cast=len(cast)),
        out_shape=[jax.ShapeDtypeStruct((bsz, t, nh * HEAD), BF16)] + cast_shapes,
        grid=(nh, bsz),
        in_specs=[pl.BlockSpec(memory_space=pltpu.SMEM),
                  lat(0), lat(1), lat(2), lat(3), ctx(0), ctx(1)] + cast_specs,
        out_specs=[pl.BlockSpec((1, t, HEAD), lambda h, b: (b, 0, h))] + cast_specs,
        scratch_shapes=[pltpu.VMEM((N_DROW, GRID_W, 2 * GRID_W), F32),
                        pltpu.VMEM((3, NA_QROWS * GRID_W, NA_KROWS * GRID_W), F32)],
        compiler_params=_params(("arbitrary", "arbitrary")),
        name="na",
    )(rpb.astype(F32).reshape(-1), proj, proj, proj, proj, proj_ctx, proj_ctx, *cast)


def _seg_cumsum(x, reverse):
    r_i = lax.broadcasted_iota(jnp.int32, (CHUNK, 2 * CHUNK), 0)
    c_i = lax.broadcasted_iota(jnp.int32, (CHUNK, 2 * CHUNK), 1) & (CHUNK - 1)
    tri2 = jnp.where((r_i <= c_i) if reverse else (r_i >= c_i), 1.0, 0.0).astype(BF16)
    hi = x.astype(BF16)
    lo = (x - hi.astype(F32)).astype(BF16)
    out = []
    for c in range(x.shape[0] // CHUNK):
        rows = slice(c * CHUNK, (c + 1) * CHUNK)
        out.append(jnp.dot(tri2, jnp.concatenate([hi[rows], lo[rows]], axis=0), preferred_element_type=F32))
    return jnp.concatenate(out, axis=0)


def _forget(f_pre, lb):
    sg = jax.nn.sigmoid(f_pre)
    one_m_lb = 1.0 - lb
    return one_m_lb * (1.0 - sg), jnp.log(lb + one_m_lb * sg)


def _lower_bound(logits):
    m = jnp.max(logits, axis=0, keepdims=True)
    e = jnp.exp(logits - m)
    return e[0:1] / jnp.sum(e, axis=0, keepdims=True)


def _gates(f_pre, lb, reverse):
    kk, logf = _forget(f_pre, lb)
    cum = _seg_cumsum(logf, reverse).reshape(-1, CHUNK, HEAD)
    mid = CHUNK // 2
    ref = cum[:, mid:mid + 1] if reverse else cum[:, mid - 1:mid]
    last = cum[:, 0:1] if reverse else cum[:, CHUNK - 1:CHUNK]
    return kk.reshape(cum.shape), cum, ref, last


def _decay_columns(dec):
    pad = jnp.zeros((HEAD - dec.shape[0], HEAD), F32)
    return jnp.concatenate([dec, pad], axis=0).T


def _ctx_state(f_pre, v, lb, reverse):
    kk, cum, _, last = _gates(f_pre, lb, reverse)
    kl = (kk * jnp.exp(last - cum)).astype(BF16)
    dec_t = _decay_columns(jnp.exp(last).reshape(-1, HEAD))
    tn = (((0,), (0,)), ((), ()))
    n_chunks = kl.shape[0]
    st = jnp.zeros((HEAD, HEAD), F32)
    for c in (reversed(range(n_chunks)) if reverse else range(n_chunks)):
        u = lax.dot_general(kl[c], v[c * CHUNK:(c + 1) * CHUNK], tn, preferred_element_type=F32)
        st = st * dec_t[:, c:c + 1] + u
    return st


def _hg_prepare(qs, f_pre, lb, reverse):
    kk, cum, ref, last = _gates(f_pre, lb, reverse)
    e = jnp.exp(cum - ref)
    qd = qs.reshape(cum.shape) * e
    kd = kk / e
    qe = qd * jnp.exp(ref)
    kl = kd * jnp.exp(last - ref)
    flat = lambda a: a.reshape(-1, HEAD).astype(BF16)
    return flat(qd), flat(kd), flat(qe), flat(kl), jnp.exp(last).reshape(-1, HEAD)


def _hg_kernel(q_ref, ff_ref, fb_ref, i_ref, g_ref, ffc_ref, fbc_ref, ic_ref, lbf_ref, lbb_ref, nw_ref,
               *refs, t, n_cast):
    o_ref = refs[n_cast]
    qs_s, qd_s, kd_s, qe_s, kl_s, dec_s, ob_s = refs[2 * n_cast + 1:]
    _cast_slabs(refs[:n_cast], refs[n_cast + 1:2 * n_cast + 1])
    n_tiles = t // HG_TILE
    lb_f = _lower_bound(lbf_ref[...])
    lb_b = _lower_bound(lbb_ref[...])
    nw = nw_ref[...]
    r_i = lax.broadcasted_iota(jnp.int32, (CHUNK, CHUNK), 0)
    c_i = lax.broadcasted_iota(jnp.int32, (CHUNK, CHUNK), 1)
    nt = (((1,), (1,)), ((), ()))
    tn = (((0,), (0,)), ((), ()))

    def prepare(tix, f_ref, lb, reverse, first):
        rs = pl.ds(pl.multiple_of(tix * HG_TILE, HG_TILE), HG_TILE)
        if first:
            q = q_ref[0, 0, rs, :].astype(F32)
            qs = q * jax.nn.sigmoid(q)
        else:
            qs = qs_s[rs, :]
        return (qs if first else None,) + _hg_prepare(qs, f_ref[0, 0, rs, :].astype(F32), lb, reverse)

    def stage(tix, vals):
        rs = pl.ds(pl.multiple_of(tix * HG_TILE, HG_TILE), HG_TILE)
        qs, qd, kd, qe, kl, dec = vals
        if qs is not None:
            qs_s[rs, :] = qs
        qd_s[rs, :] = qd
        kd_s[rs, :] = kd
        qe_s[rs, :] = qe
        kl_s[rs, :] = kl
        dec_s[tix] = dec

    def scan(tix, st, reverse, emit):
        tri = (r_i <= c_i) if reverse else (r_i >= c_i)
        order = [HG_CHUNKS - 1 - u if reverse else u for u in range(HG_CHUNKS)]
        cs = {j: pl.ds(pl.multiple_of(tix * HG_TILE + j * CHUNK, CHUNK), CHUNK) for j in order}
        v = {j: i_ref[0, 0, cs[j], :] for j in order}
        a = {j: lax.dot_general(qd_s[cs[j], :], kd_s[cs[j], :], nt, preferred_element_type=F32) for j in order}
        u = {j: lax.dot_general(kl_s[cs[j], :], v[j], tn, preferred_element_type=F32) for j in order}
        dec_t = _decay_columns(dec_s[tix])
        dec_b = {j: jnp.broadcast_to(dec_t[:, j:j + 1], (HEAD, HEAD)) for j in order}
        for j in order:
            lhs = jnp.concatenate([qe_s[cs[j], :], jnp.where(tri, a[j], 0.0).astype(BF16)], axis=1)
            rhs = jnp.concatenate([st.astype(BF16), v[j]], axis=0)
            emit(cs[j], jnp.dot(lhs, rhs, preferred_element_type=F32))
            st = st * dec_b[j] + u[j]
        return st

    def direction(f_ref, fc_ref, lb, reverse, first, emit):
        tile_of = (lambda n: n_tiles - 1 - n) if reverse else (lambda n: n)
        stage(tile_of(0), prepare(tile_of(0), f_ref, lb, reverse, first))
        st = _ctx_state(fc_ref[0, 0].astype(F32), ic_ref[0, 0], lb, reverse)

        def trip(n, st):
            nxt = tile_of(jnp.minimum(n + 1, n_tiles - 1))
            vals = prepare(nxt, f_ref, lb, reverse, first)
            st = scan(tile_of(n), st, reverse, emit)
            stage(nxt, vals)
            return st

        lax.fori_loop(0, n_tiles, trip, st)

    def emit_backward(cs, o):
        ob_s[cs, :] = o

    def emit_forward(cs, o):
        o = o + ob_s[cs, :]
        o = o * lax.rsqrt(jnp.mean(o * o, axis=-1, keepdims=True) + RMS_EPS) * nw
        g = g_ref[0, 0, cs, :].astype(F32)
        o_ref[0, cs, :] = (o * (g * jax.nn.sigmoid(g))).astype(BF16)

    direction(fb_ref, fbc_ref, lb_b, True, True, emit_backward)
    direction(ff_ref, ffc_ref, lb_f, False, False, emit_forward)


def _hgrn2(proj, proj_ctx, lb_f, lb_b, norm_w, nh, cast):
    bsz, _, t, _ = proj.shape
    lc = proj_ctx.shape[2]
    n_slots = lb_f.shape[0]
    assert t % HG_TILE == 0 and lc % CHUNK == 0
    lat = lambda g: pl.BlockSpec((1, 1, t, HEAD), lambda b, h: (b, g * nh + h, 0, 0))
    ctx = lambda g: pl.BlockSpec((1, 1, lc, HEAD), lambda b, h: (g * nh + h, b, 0, 0))
    lbs = pl.BlockSpec((n_slots, HEAD), lambda b, h: (0, h))
    staged = pltpu.VMEM((t, HEAD), BF16)
    cast_specs, cast_shapes = _cast_specs(cast, bsz * nh, lambda b, h: b * nh + h)
    return pl.pallas_call(
        functools.partial(_hg_kernel, t=t, n_cast=len(cast)),
        out_shape=[jax.ShapeDtypeStruct((bsz, t, nh * HEAD), BF16)] + cast_shapes,
        grid=(bsz, nh),
        in_specs=[lat(4), lat(5), lat(6), lat(7), lat(8), ctx(2), ctx(3), ctx(4), lbs, lbs,
                  pl.BlockSpec((1, HEAD), lambda b, h: (0, 0))] + cast_specs,
        out_specs=[pl.BlockSpec((1, t, HEAD), lambda b, h: (b, 0, h))] + cast_specs,
        scratch_shapes=[pltpu.VMEM((t, HEAD), F32), staged, staged, staged, staged,
                        pltpu.VMEM((t // HG_TILE, HG_CHUNKS, HEAD), F32), pltpu.VMEM((t, HEAD), F32)],
        compiler_params=_params(("arbitrary", "arbitrary")),
        name="hgrn2",
    )(proj, proj, proj, proj, proj, proj_ctx, proj_ctx, proj_ctx, lb_f, lb_b, norm_w, *cast)


def _merge_kernel(ya_ref, yb_ref, wa_ref, wb_ref, ga_ref, gb_ref, o_ref):
    ta = jnp.dot(ya_ref[0], wa_ref[...], preferred_element_type=F32)
    tb = jnp.dot(yb_ref[0], wb_ref[...], preferred_element_type=F32)
    for s in range(ga_ref.shape[1]):
        cs = slice(s * HEAD, (s + 1) * HEAD)
        m = (jax.nn.sigmoid(ga_ref[0, s].astype(F32)) * ta[:, cs]
             + jax.nn.sigmoid(gb_ref[0, s].astype(F32)) * tb[:, cs])
        o_ref[0, :, cs] = m.astype(BF16)


def _merge(ya, yb, w_pa, w_pb, proj, nh):
    bsz, t, _ = ya.shape
    d = w_pa.shape[1]
    tm = _tile(t, 1024)
    tn = _tile(nh * HEAD, 1024)
    sub = tn // HEAD
    ga0 = 9 * nh // sub
    gb0 = 11 * nh // sub
    assert (9 * nh) % sub == 0 and (11 * nh) % sub == 0
    yspec = pl.BlockSpec((1, tm, nh * HEAD), lambda b, i, j: (b, i, 0))
    wspec = pl.BlockSpec((nh * HEAD, tn), lambda b, i, j: (0, j))
    return pl.pallas_call(
        _merge_kernel,
        out_shape=jax.ShapeDtypeStruct((bsz, t, d), BF16),
        grid=(bsz, t // tm, d // tn),
        in_specs=[yspec, yspec, wspec, wspec,
                  pl.BlockSpec((1, sub, tm, HEAD), lambda b, i, j: (b, ga0 + j, i, 0)),
                  pl.BlockSpec((1, sub, tm, HEAD), lambda b, i, j: (b, gb0 + j, i, 0))],
        out_specs=pl.BlockSpec((1, tm, tn), lambda b, i, j: (b, i, j)),
        compiler_params=_params(("arbitrary", "arbitrary", "arbitrary")),
        name="merge",
    )(ya, yb, w_pa, w_pb, proj, proj)


def _out_kernel(m_ref, w_ref, x_ref, gate_ref, lng_ref, lnb_ref, o_ref, *, alpha, tn):
    j = pl.program_id(2)
    out = jnp.dot(m_ref[0], w_ref[...], preferred_element_type=F32)
    col = pl.multiple_of(j * tn, tn)
    o_ref[0, :, pl.ds(col, tn)] = alpha * x_ref[0] + gate_ref[0] * out

    @pl.when(j == pl.num_programs(2) - 1)
    def _():
        def norm_rows(i, carry):
            rs = pl.ds(pl.multiple_of(i * LN_ROWS, LN_ROWS), LN_ROWS)
            r = o_ref[0, rs, :]
            mu = jnp.mean(r, axis=-1, keepdims=True)
            rc = r - mu
            var = jnp.mean(rc * rc, axis=-1, keepdims=True)
            o_ref[0, rs, :] = rc * lax.rsqrt(var + LN_EPS) * lng_ref[...] + lnb_ref[...]
            return carry

        lax.fori_loop(0, o_ref.shape[1] // LN_ROWS, norm_rows, 0)


def _out(m, w_out, x, gate, ln_g, ln_b, alpha):
    bsz, t, d = x.shape
    tm = _tile(t, 512)
    tn = _tile(d, 1024)
    return pl.pallas_call(
        functools.partial(_out_kernel, alpha=alpha, tn=tn),
        out_shape=jax.ShapeDtypeStruct((bsz, t, d), F32),
        grid=(bsz, t // tm, d // tn),
        in_specs=[pl.BlockSpec((1, tm, d), lambda b, i, j: (b, i, 0)),
                  pl.BlockSpec((d, tn), lambda b, i, j: (0, j)),
                  pl.BlockSpec((1, tm, tn), lambda b, i, j: (b, i, j)),
                  pl.BlockSpec((1, 1, tn), lambda b, i, j: (b, 0, j)),
                  pl.BlockSpec((1, d), lambda b, i, j: (0, 0)),
                  pl.BlockSpec((1, d), lambda b, i, j: (0, 0))],
        out_specs=pl.BlockSpec((1, tm, d), lambda b, i, j: (b, i, 0)),
        compiler_params=_params(("arbitrary", "arbitrary", "arbitrary")),
        name="out",
    )(m, w_out, x, gate, ln_g, ln_b)


def _layer(x, ctx, cond, w_ada, b_ada, w_in, rpb, lb_f, lb_b, norm_w, w_pa, w_pb, w_out, ln_g, ln_b, alpha):
    bsz, t, d = x.shape
    nh = (d // 2) // HEAD
    grp = d // 2
    tn = _tile(grp, 1024)
    per = grp // tn

    mod = _adaln(cond, w_ada, b_ada.reshape(1, -1))
    shift, scale, gate = (mod[:3, k * d:(k + 1) * d].reshape(3, 1, d) for k in range(3))
    ctx_row = bsz

    h_lat = _ln_mod(x, shift, scale, lambda b: b)
    h_ctx = _ln_mod(ctx, shift, scale, lambda b: ctx_row)

    first, w_in16 = _proj_first(h_lat, w_in)
    proj = _proj_main(h_lat, w_in16, first, tn)
    proj_ctx = _proj(h_ctx.reshape(1, -1, d), w_in16, 5 * per,
                     lambda j: jnp.where(j < 2 * per, j + per, j + 3 * per), tn, "proj_ctx")
    proj_ctx = proj_ctx.reshape(5 * nh, bsz, ctx.shape[1], HEAD)

    y_a, w_pa16, w_pb16 = _na(proj, proj_ctx, rpb, nh, (w_pa, w_pb))
    y_b, w_out16 = _hgrn2(proj, proj_ctx, lb_f, lb_b, norm_w.reshape(1, HEAD), nh, (w_out,))
    m = _merge(y_a, y_b, w_pa16, w_pb16, proj, nh)
    return _out(m, w_out16, x, gate, ln_g.reshape(1, d), ln_b.reshape(1, d), alpha)


def kernel(x, c, ctx, c_ctx, w_ada, b_ada, w_in, na_rpb, hg_lb_fwd, hg_lb_bwd, hg_norm_w, w_pa, w_pb, w_out,
           ln_g, ln_b):
    depth = w_ada.shape[0]
    assert depth == 1, "the context stream update of deeper stacks is not implemented"
    bsz, _, d = x.shape
    alpha = (2.0 * depth) ** 0.25
    cond = jnp.concatenate([c, c_ctx[None], jnp.zeros((8 - bsz - 1, d), c.dtype)], axis=0)
    return _layer(x, ctx, cond, w_ada[0], b_ada[0], w_in[0], na_rpb[0], hg_lb_fwd, hg_lb_bwd, hg_norm_w[0],
                  w_pa[0], w_pb[0], w_out[0], ln_g[0], ln_b[0], alpha)
```

```python
import functools
import math

import jax
import jax.numpy as jnp
from jax import lax
from jax.experimental import pallas as pl
from jax.experimental.pallas import tpu as pltpu

F32 = jnp.float32
BF16 = jnp.bfloat16

GRID_W = 64
HEAD = 128
NA_KH = 8
NA_KW = 16
NA_QROWS = 4
NA_KROWS = 12
NA_UNROLL = 4
CHUNK = 64
HG_CHUNKS = 16
HG_TILE = HG_CHUNKS * CHUNK
LN_ROWS = 64
OUT_W_SLOTS = 3
LOG2E = math.log2(math.e)
LN_EPS = 1e-6
RMS_EPS = 1e-6
VMEM_LIMIT = 56 * 1024 * 1024
OUT_VMEM_LIMIT = 58 * 1024 * 1024


def _params(semantics, vmem=VMEM_LIMIT):
    return pltpu.CompilerParams(dimension_semantics=semantics, vmem_limit_bytes=vmem)


def _tile(n, pref):
    t = min(n, pref)
    assert n % t == 0, (n, t)
    return t


def _adaln_kernel(c_ref, w_ref, b_ref, o_ref):
    c = c_ref[...]
    s = c * jax.nn.sigmoid(c)
    s_hi = s.astype(BF16)
    s_lo = (s - s_hi.astype(F32)).astype(BF16)
    w = w_ref[...]
    w_hi = w.astype(BF16)
    w_lo = (w - w_hi.astype(F32)).astype(BF16)
    n = s.shape[0]
    lhs = jnp.concatenate([s_hi, s_lo], axis=0)
    a = jnp.dot(lhs, w_hi, preferred_element_type=F32)
    b = jnp.dot(s_hi, w_lo, preferred_element_type=F32)
    o_ref[...] = a[:n] + a[n:] + b + b_ref[...]


def _adaln(cond, w, b):
    d, n = w.shape
    tn = _tile(n, 512)
    return pl.pallas_call(
        _adaln_kernel,
        out_shape=jax.ShapeDtypeStruct((cond.shape[0], n), F32),
        grid=(n // tn,),
        in_specs=[pl.BlockSpec(cond.shape, lambda j: (0, 0)),
                  pl.BlockSpec((d, tn), lambda j: (0, j)),
                  pl.BlockSpec((1, tn), lambda j: (0, j))],
        out_specs=pl.BlockSpec((cond.shape[0], tn), lambda j: (0, j)),
        compiler_params=_params(("arbitrary",)),
        name="adaln",
    )(cond, w, b)


def _ln_mod_kernel(x_ref, shift_ref, scale_ref, o_ref):
    x = x_ref[0]
    mu = jnp.mean(x, axis=-1, keepdims=True)
    xc = x - mu
    var = jnp.mean(xc * xc, axis=-1, keepdims=True)
    y = xc * lax.rsqrt(var + LN_EPS)
    o_ref[0] = (y * (1.0 + scale_ref[0]) + shift_ref[0]).astype(BF16)


def _ln_mod(x, shift, scale, cond_row):
    bsz, t, d = x.shape
    tm = _tile(t, 512)
    cmap = lambda b, i: (cond_row(b), 0, 0)
    return pl.pallas_call(
        _ln_mod_kernel,
        out_shape=jax.ShapeDtypeStruct((bsz, t, d), BF16),
        grid=(bsz, t // tm),
        in_specs=[pl.BlockSpec((1, tm, d), lambda b, i: (b, i, 0)),
                  pl.BlockSpec((1, 1, d), cmap),
                  pl.BlockSpec((1, 1, d), cmap)],
        out_specs=pl.BlockSpec((1, tm, d), lambda b, i: (b, i, 0)),
        compiler_params=_params(("arbitrary", "arbitrary")),
        name="ln_mod",
    )(x, shift, scale)


def _store_head_major(o_ref, acc):
    for s in range(o_ref.shape[1]):
        o_ref[0, s] = acc[:, s * HEAD:(s + 1) * HEAD].astype(BF16)


def _proj_kernel(h_ref, w_ref, o_ref):
    _store_head_major(o_ref, jnp.dot(h_ref[0], w_ref[...], preferred_element_type=F32))


def _cast_specs(weights, n_steps, step):
    specs, shapes = [], []
    for w in weights:
        rows, cols = w.shape
        split = next(s for s in range(n_steps, 0, -1) if rows % s == 0 and (rows // s) % 16 == 0)
        specs.append(pl.BlockSpec((rows // split, cols),
                                  lambda *ids, split=split: (jnp.minimum(step(*ids), split - 1), 0)))
        shapes.append(jax.ShapeDtypeStruct(w.shape, BF16))
    return specs, shapes


def _cast_slabs(srcs, dsts):
    for src, dst in zip(srcs, dsts):
        dst[...] = src[...].astype(BF16)


def _proj_cast_kernel(h_ref, w_ref, o_ref, w16_ref):
    w16 = w_ref[...].astype(BF16)
    w16_ref[...] = w16
    _store_head_major(o_ref, jnp.dot(h_ref[0], w16, preferred_element_type=F32))


def _proj_first(h, w):
    _, t, d = h.shape
    n = w.shape[1]
    tm = _tile(t, 1024)
    tn = next(c for c in (512, 256, HEAD) if n % c == 0)
    sub = tn // HEAD
    return pl.pallas_call(
        _proj_cast_kernel,
        out_shape=(jax.ShapeDtypeStruct((1, n // HEAD, tm, HEAD), BF16), jax.ShapeDtypeStruct((d, n), BF16)),
        grid=(n // tn,),
        in_specs=[pl.BlockSpec((1, tm, d), lambda j: (0, 0, 0)),
                  pl.BlockSpec((d, tn), lambda j: (0, j))],
        out_specs=[pl.BlockSpec((1, sub, tm, HEAD), lambda j: (0, j, 0, 0)),
                   pl.BlockSpec((d, tn), lambda j: (0, j))],
        compiler_params=_params(("arbitrary",)),
        name="proj_first",
    )(h, w)


def _proj(h, w, n_tiles, col_tile, tn, name):
    bsz, t, d = h.shape
    tm = _tile(t, 1024)
    sub = tn // HEAD
    return pl.pallas_call(
        _proj_kernel,
        out_shape=jax.ShapeDtypeStruct((bsz, n_tiles * sub, t, HEAD), BF16),
        grid=(bsz, t // tm, n_tiles),
        in_specs=[pl.BlockSpec((1, tm, d), lambda b, i, j: (b, i, 0)),
                  pl.BlockSpec((d, tn), lambda b, i, j: (0, col_tile(j)))],
        out_specs=pl.BlockSpec((1, sub, tm, HEAD), lambda b, i, j: (b, j, i, 0)),
        compiler_params=_params(("arbitrary", "arbitrary", "arbitrary")),
        name=name,
    )(h, w)


def _proj_main_kernel(h_ref, w_ref, first_ref, o_ref, *, tm):
    rp = pl.program_id(1)

    @pl.when(rp == 0)
    def _():
        o_ref[0, :, 0:tm, :] = first_ref[0]

    acc = jnp.dot(h_ref[0], w_ref[...], preferred_element_type=F32)
    rows = pl.ds(pl.multiple_of(((rp + 1) % 2) * tm, tm), tm)
    for s in range(o_ref.shape[1]):
        o_ref[0, s, rows, :] = acc[:, s * HEAD:(s + 1) * HEAD].astype(BF16)


def _proj_main(h, w, first, tn):
    bsz, t, d = h.shape
    n = w.shape[1]
    tm = _tile(t, 1024)
    nb = t // tm
    sub = tn // HEAD
    assert nb % 2 == 0 and first.shape == (1, n // HEAD, tm, HEAD)
    blk = lambda rp: ((rp + 1) // nb, (rp + 1) % nb)
    return pl.pallas_call(
        functools.partial(_proj_main_kernel, tm=tm),
        out_shape=jax.ShapeDtypeStruct((bsz, n // HEAD, t, HEAD), BF16),
        grid=(n // tn, bsz * nb - 1),
        in_specs=[pl.BlockSpec((1, tm, d), lambda j, rp: (*blk(rp), 0)),
                  pl.BlockSpec((d, tn), lambda j, rp: (0, j)),
                  pl.BlockSpec((1, sub, tm, HEAD), lambda j, rp: (0, j, 0, 0))],
        out_specs=pl.BlockSpec((1, sub, 2 * tm, HEAD), lambda j, rp: (blk(rp)[0], j, blk(rp)[1] // 2, 0)),
        compiler_params=_params(("arbitrary", "arbitrary")),
        name="proj_lat",
    )(h, w, first)


N_DROW = 2 * NA_KH - 1
N_DCOL = 2 * NA_KW - 1


def _na_block_kinds(rows):
    return ((0, 0), (NA_QROWS, 0), (rows - NA_QROWS, rows - NA_KROWS))


def _build_bias(rpb_ref, head, band_s, bias_s, rows):
    shape = (GRID_W, 2 * GRID_W)
    qc = lax.broadcasted_iota(jnp.int32, shape, 0)
    lane = lax.broadcasted_iota(jnp.int32, shape, 1)
    kc = lane & (GRID_W - 1)
    d_col = jnp.clip(kc - qc, -(NA_KW - 1), NA_KW - 1) + (NA_KW - 1)
    col_start = jnp.clip(qc - NA_KW // 2, 0, GRID_W - NA_KW)
    col_off = kc - col_start
    neg = jnp.full(shape, -jnp.inf, F32)
    base = head * (N_DROW * N_DCOL)
    for d in range(N_DROW):
        acc = neg
        for j in range(N_DCOL):
            acc = jnp.where(d_col == j, rpb_ref[base + d * N_DCOL + j] * LOG2E, acc)
        band_s[d] = jnp.where(col_off < 0, neg, jnp.where(col_off < NA_KW, acc, neg))
    left = lane < GRID_W
    for kind, (r0, kb) in enumerate(_na_block_kinds(rows)):
        for qi in range(NA_QROWS):
            r = r0 + qi
            row_start = min(max(r - NA_KH // 2, 0), rows - NA_KH)

            def band(kr):
                return band_s[kr - r + NA_KH - 1] if row_start <= kr < row_start + NA_KH else neg

            for jp in range(NA_KROWS // 2):
                kr = kb + 2 * jp
                bias_s[kind, qi * GRID_W:(qi + 1) * GRID_W, jp * 2 * GRID_W:(jp + 1) * 2 * GRID_W] = (
                    jnp.where(left, band(kr), band(kr + 1)))


def _na_kernel(rpb_ref, q_ref, k_ref, v_ref, z_ref, kc_ref, vc_ref, *refs, rows, n_cast):
    o_ref = refs[n_cast]
    band_s, bias_s = refs[2 * n_cast + 1:]
    _cast_slabs(refs[:n_cast], refs[n_cast + 1:2 * n_cast + 1])
    nq = NA_QROWS * GRID_W
    nk = NA_KROWS * GRID_W
    n_blocks = rows // NA_QROWS
    scale = HEAD ** -0.5 * LOG2E
    kc = kc_ref[0, 0]
    vc = vc_ref[0, 0]
    nt = (((1,), (1,)), ((), ()))

    @pl.when(pl.program_id(1) == 0)
    def _():
        _build_bias(rpb_ref, pl.program_id(0), band_s, bias_s, rows)

    def scores(i):
        r0 = i * NA_QROWS
        kb = jnp.clip(r0 - NA_KH // 2, 0, rows - NA_KROWS)
        kind = jnp.where(i == 0, 0, jnp.where(i == n_blocks - 1, 2, 1))
        q_off = pl.multiple_of(r0 * GRID_W, GRID_W)
        k_off = pl.multiple_of(kb * GRID_W, GRID_W)
        q = (q_ref[0, 0, pl.ds(q_off, nq), :].astype(F32) * scale).astype(BF16)
        s_loc = lax.dot_general(q, k_ref[0, 0, pl.ds(k_off, nk), :], nt, preferred_element_type=F32)
        s_ctx = lax.dot_general(q, kc, nt, preferred_element_type=F32)
        return q_off, k_off, s_loc + bias_s[kind], s_ctx

    def attend(q_off, k_off, s_loc, s_ctx):
        m = jnp.maximum(jnp.max(s_loc, axis=-1, keepdims=True), jnp.max(s_ctx, axis=-1, keepdims=True))
        p_loc = jnp.exp2(s_loc - m)
        p_ctx = jnp.exp2(s_ctx - m)
        denom = jnp.sum(p_loc, axis=-1, keepdims=True) + jnp.sum(p_ctx, axis=-1, keepdims=True)
        o = (jnp.dot(p_loc.astype(BF16), v_ref[0, 0, pl.ds(k_off, nk), :], preferred_element_type=F32)
             + jnp.dot(p_ctx.astype(BF16), vc, preferred_element_type=F32))
        o = o / denom
        z = z_ref[0, 0, pl.ds(q_off, nq), :].astype(F32)
        o_ref[0, pl.ds(q_off, nq), :] = (o * (z * jax.nn.sigmoid(z))).astype(BF16)

    def blocks(n, carry):
        staged = [scores(n * NA_UNROLL + u) for u in range(NA_UNROLL)]
        for args in staged:
            attend(*args)
        return carry

    lax.fori_loop(0, n_blocks // NA_UNROLL, blocks, 0)


def _na(proj, proj_ctx, rpb, nh, cast):
    bsz, _, t, _ = proj.shape
    lc = proj_ctx.shape[2]
    rows = t // GRID_W
    assert rows % (NA_QROWS * NA_UNROLL) == 0 and rows >= NA_KROWS + NA_QROWS
    assert rpb.shape == (nh, N_DROW, N_DCOL)
    lat = lambda g: pl.BlockSpec((1, 1, t, HEAD), lambda h, b: (b, g * nh + h, 0, 0))
    ctx = lambda g: pl.BlockSpec((1, 1, lc, HEAD), lambda h, b: (g * nh + h, b, 0, 0))
    cast_specs, cast_shapes = _cast_specs(cast, nh * bsz, lambda h, b: h * bsz + b)
    return pl.pallas_call(
        functools.partial(_na_kernel, rows=rows, n_cast=len(cast)),
        out_shape=[jax.ShapeDtypeStruct((bsz, t, nh * HEAD), BF16)] + cast_shapes,
        grid=(nh, bsz),
        in_specs=[pl.BlockSpec(memory_space=pltpu.SMEM),
                  lat(0), lat(1), lat(2), lat(3), ctx(0), ctx(1)] + cast_specs,
        out_specs=[pl.BlockSpec((1, t, HEAD), lambda h, b: (b, 0, h))] + cast_specs,
        scratch_shapes=[pltpu.VMEM((N_DROW, GRID_W, 2 * GRID_W), F32),
                        pltpu.VMEM((3, NA_QROWS * GRID_W, NA_KROWS * GRID_W), F32)],
        compiler_params=_params(("arbitrary", "arbitrary")),
        name="na",
    )(rpb.astype(F32).reshape(-1), proj, proj, proj, proj, proj_ctx, proj_ctx, *cast)


def _seg_cumsum(x, reverse):
    r_i = lax.broadcasted_iota(jnp.int32, (CHUNK, 2 * CHUNK), 0)
    c_i = lax.broadcasted_iota(jnp.int32, (CHUNK, 2 * CHUNK), 1) & (CHUNK - 1)
    tri2 = jnp.where((r_i <= c_i) if reverse else (r_i >= c_i), 1.0, 0.0).astype(BF16)
    hi = x.astype(BF16)
    lo = (x - hi.astype(F32)).astype(BF16)
    out = []
    for c in range(x.shape[0] // CHUNK):
        rows = slice(c * CHUNK, (c + 1) * CHUNK)
        out.append(jnp.dot(tri2, jnp.concatenate([hi[rows], lo[rows]], axis=0), preferred_element_type=F32))
    return jnp.concatenate(out, axis=0)


def _forget(f_pre, lb):
    sg = jax.nn.sigmoid(f_pre)
    one_m_lb = 1.0 - lb
    return one_m_lb * (1.0 - sg), jnp.log(lb + one_m_lb * sg)


def _lower_bound(logits):
    m = jnp.max(logits, axis=0, keepdims=True)
    e = jnp.exp(logits - m)
    return e[0:1] / jnp.sum(e, axis=0, keepdims=True)


def _gates(f_pre, lb, reverse):
    kk, logf = _forget(f_pre, lb)
    cum = _seg_cumsum(logf, reverse).reshape(-1, CHUNK, HEAD)
    mid = CHUNK // 2
    ref = cum[:, mid:mid + 1] if reverse else cum[:, mid - 1:mid]
    last = cum[:, 0:1] if reverse else cum[:, CHUNK - 1:CHUNK]
    return kk.reshape(cum.shape), cum, ref, last


def _decay_columns(dec):
    pad = jnp.zeros((HEAD - dec.shape[0], HEAD), F32)
    return jnp.concatenate([dec, pad], axis=0).T


def _ctx_state(f_pre, v, lb, reverse):
    kk, cum, _, last = _gates(f_pre, lb, reverse)
    kl = (kk * jnp.exp(last - cum)).astype(BF16)
    dec_t = _decay_columns(jnp.exp(last).reshape(-1, HEAD))
    tn = (((0,), (0,)), ((), ()))
    n_chunks = kl.shape[0]
    st = jnp.zeros((HEAD, HEAD), F32)
    for c in (reversed(range(n_chunks)) if reverse else range(n_chunks)):
        u = lax.dot_general(kl[c], v[c * CHUNK:(c + 1) * CHUNK], tn, preferred_element_type=F32)
        st = st * dec_t[:, c:c + 1] + u
    return st


def _hg_prepare(qs, f_pre, lb, reverse):
    kk, cum, ref, last = _gates(f_pre, lb, reverse)
    e = jnp.exp(cum - ref)
    qd = qs.reshape(cum.shape) * e
    kd = kk / e
    qe = qd * jnp.exp(ref)
    kl = kd * jnp.exp(last - ref)
    flat = lambda a: a.reshape(-1, HEAD).astype(BF16)
    return flat(qd), flat(kd), flat(qe), flat(kl), jnp.exp(last).reshape(-1, HEAD)


def _hg_kernel(q_ref, ff_ref, fb_ref, i_ref, g_ref, ffc_ref, fbc_ref, ic_ref, lbf_ref, lbb_ref, nw_ref,
               *refs, t, n_cast):
    o_ref = refs[n_cast]
    qs_s, qd_s, kd_s, qe_s, kl_s, dec_s, ob_s = refs[2 * n_cast + 1:]
    _cast_slabs(refs[:n_cast], refs[n_cast + 1:2 * n_cast + 1])
    n_tiles = t // HG_TILE
    lb_f = _lower_bound(lbf_ref[...])
    lb_b = _lower_bound(lbb_ref[...])
    nw = nw_ref[...]
    r_i = lax.broadcasted_iota(jnp.int32, (CHUNK, CHUNK), 0)
    c_i = lax.broadcasted_iota(jnp.int32, (CHUNK, CHUNK), 1)
    nt = (((1,), (1,)), ((), ()))
    tn = (((0,), (0,)), ((), ()))

    def prepare(tix, f_ref, lb, reverse, first):
        rs = pl.ds(pl.multiple_of(tix * HG_TILE, HG_TILE), HG_TILE)
        if first:
            q = q_ref[0, 0, rs, :].astype(F32)
            qs = q * jax.nn.sigmoid(q)
        else:
            qs = qs_s[rs, :]
        return (qs if first else None,) + _hg_prepare(qs, f_ref[0, 0, rs, :].astype(F32), lb, reverse)

    def stage(tix, vals):
        rs = pl.ds(pl.multiple_of(tix * HG_TILE, HG_TILE), HG_TILE)
        qs, qd, kd, qe, kl, dec = vals
        if qs is not None:
            qs_s[rs, :] = qs
        qd_s[rs, :] = qd
        kd_s[rs, :] = kd
        qe_s[rs, :] = qe
        kl_s[rs, :] = kl
        dec_s[tix] = dec

    def scan(tix, st, reverse, emit):
        tri = (r_i <= c_i) if reverse else (r_i >= c_i)
        order = [HG_CHUNKS - 1 - u if reverse else u for u in range(HG_CHUNKS)]
        cs = {j: pl.ds(pl.multiple_of(tix * HG_TILE + j * CHUNK, CHUNK), CHUNK) for j in order}
        v = {j: i_ref[0, 0, cs[j], :] for j in order}
        a = {j: lax.dot_general(qd_s[cs[j], :], kd_s[cs[j], :], nt, preferred_element_type=F32) for j in order}
        u = {j: lax.dot_general(kl_s[cs[j], :], v[j], tn, preferred_element_type=F32) for j in order}
        dec_t = _decay_columns(dec_s[tix])
        dec_b = {j: jnp.broadcast_to(dec_t[:, j:j + 1], (HEAD, HEAD)) for j in order}
        for j in order:
            lhs = jnp.concatenate([qe_s[cs[j], :], jnp.where(tri, a[j], 0.0).astype(BF16)], axis=1)
            rhs = jnp.concatenate([st.astype(BF16), v[j]], axis=0)
            emit(cs[j], jnp.dot(lhs, rhs, preferred_element_type=F32))
            st = st * dec_b[j] + u[j]
        return st

    def direction(f_ref, fc_ref, lb, reverse, first, emit):
        tile_of = (lambda n: n_tiles - 1 - n) if reverse else (lambda n: n)
        stage(tile_of(0), prepare(tile_of(0), f_ref, lb, reverse, first))
        st = _ctx_state(fc_ref[0, 0].astype(F32), ic_ref[0, 0], lb, reverse)

        def trip(n, st):
            nxt = tile_of(jnp.minimum(n + 1, n_tiles - 1))
            vals = prepare(nxt, f_ref, lb, reverse, first)
            st = scan(tile_of(n), st, reverse, emit)
            stage(nxt, vals)
            return st

        lax.fori_loop(0, n_tiles, trip, st)

    def emit_backward(cs, o):
        ob_s[cs, :] = o

    def emit_forward(cs, o):
        o = o + ob_s[cs, :]
        o = o * lax.rsqrt(jnp.mean(o * o, axis=-1, keepdims=True) + RMS_EPS) * nw
        g = g_ref[0, 0, cs, :].astype(F32)
        o_ref[0, cs, :] = (o * (g * jax.nn.sigmoid(g))).astype(BF16)

    direction(fb_ref, fbc_ref, lb_b, True, True, emit_backward)
    direction(ff_ref, ffc_ref, lb_f, False, False, emit_forward)


def _hgrn2(proj, proj_ctx, lb_f, lb_b, norm_w, nh, cast):
    bsz, _, t, _ = proj.shape
    lc = proj_ctx.shape[2]
    n_slots = lb_f.shape[0]
    assert t % HG_TILE == 0 and lc % CHUNK == 0
    lat = lambda g: pl.BlockSpec((1, 1, t, HEAD), lambda b, h: (b, g * nh + h, 0, 0))
    ctx = lambda g: pl.BlockSpec((1, 1, lc, HEAD), lambda b, h: (g * nh + h, b, 0, 0))
    lbs = pl.BlockSpec((n_slots, HEAD), lambda b, h: (0, h))
    staged = pltpu.VMEM((t, HEAD), BF16)
    cast_specs, cast_shapes = _cast_specs(cast, bsz * nh, lambda b, h: b * nh + h)
    return pl.pallas_call(
        functools.partial(_hg_kernel, t=t, n_cast=len(cast)),
        out_shape=[jax.ShapeDtypeStruct((bsz, t, nh * HEAD), BF16)] + cast_shapes,
        grid=(bsz, nh),
        in_specs=[lat(4), lat(5), lat(6), lat(7), lat(8), ctx(2), ctx(3), ctx(4), lbs, lbs,
                  pl.BlockSpec((1, HEAD), lambda b, h: (0, 0))] + cast_specs,
        out_specs=[pl.BlockSpec((1, t, HEAD), lambda b, h: (b, 0, h))] + cast_specs,
        scratch_shapes=[pltpu.VMEM((t, HEAD), F32), staged, staged, staged, staged,
                        pltpu.VMEM((t // HG_TILE, HG_CHUNKS, HEAD), F32), pltpu.VMEM((t, HEAD), F32)],
        compiler_params=_params(("arbitrary", "arbitrary")),
        name="hgrn2",
    )(proj, proj, proj, proj, proj, proj_ctx, proj_ctx, proj_ctx, lb_f, lb_b, norm_w, *cast)


def _merge_kernel(ya_ref, yb_ref, wa_ref, wb_ref, ga_ref, gb_ref, o_ref):
    ta = jnp.dot(ya_ref[0], wa_ref[...], preferred_element_type=F32)
    tb = jnp.dot(yb_ref[0], wb_ref[...], preferred_element_type=F32)
    for s in range(ga_ref.shape[1]):
        cs = slice(s * HEAD, (s + 1) * HEAD)
        m = (jax.nn.sigmoid(ga_ref[0, s].astype(F32)) * ta[:, cs]
             + jax.nn.sigmoid(gb_ref[0, s].astype(F32)) * tb[:, cs])
        o_ref[0, :, cs] = m.astype(BF16)


def _merge(ya, yb, w_pa, w_pb, proj, nh):
    bsz, t, _ = ya.shape
    d = w_pa.shape[1]
    tm = _tile(t, 1024)
    tn = _tile(nh * HEAD, 1024)
    sub = tn // HEAD
    ga0 = 9 * nh // sub
    gb0 = 11 * nh // sub
    assert (9 * nh) % sub == 0 and (11 * nh) % sub == 0
    yspec = pl.BlockSpec((1, tm, nh * HEAD), lambda b, i, j: (b, i, 0))
    wspec = pl.BlockSpec((nh * HEAD, tn), lambda b, i, j: (0, j))
    return pl.pallas_call(
        _merge_kernel,
        out_shape=jax.ShapeDtypeStruct((bsz, t, d), BF16),
        grid=(bsz, t // tm, d // tn),
        in_specs=[yspec, yspec, wspec, wspec,
                  pl.BlockSpec((1, sub, tm, HEAD), lambda b, i, j: (b, ga0 + j, i, 0)),
                  pl.BlockSpec((1, sub, tm, HEAD), lambda b, i, j: (b, gb0 + j, i, 0))],
        out_specs=pl.BlockSpec((1, tm, tn), lambda b, i, j: (b, i, j)),
        compiler_params=_params(("arbitrary", "arbitrary", "arbitrary")),
        name="merge",
    )(ya, yb, w_pa, w_pb, proj, proj)


def _out_kernel(m_ref, w_hbm, x_ref, gate_ref, lng_ref, lnb_ref, o_ref, w_buf, w_sem, *, alpha, tn):
    n_col = pl.num_programs(2)
    j = pl.program_id(2)
    step = (pl.program_id(0) * pl.num_programs(1) + pl.program_id(1)) * n_col + j
    n_steps = pl.num_programs(0) * pl.num_programs(1) * n_col

    def tile_copy(s):
        src = w_hbm.at[:, pl.ds(pl.multiple_of((s % n_col) * tn, tn), tn)]
        return pltpu.make_async_copy(src, w_buf.at[s % OUT_W_SLOTS], w_sem.at[s % OUT_W_SLOTS])

    @pl.when(step == 0)
    def _():
        for s in range(OUT_W_SLOTS - 1):
            tile_copy(s).start()

    @pl.when(step + OUT_W_SLOTS - 1 < n_steps)
    def _():
        tile_copy(step + OUT_W_SLOTS - 1).start()

    tile_copy(step).wait()
    out = jnp.dot(m_ref[0], w_buf[step % OUT_W_SLOTS], preferred_element_type=F32)
    col = pl.multiple_of(j * tn, tn)
    o_ref[0, :, pl.ds(col, tn)] = alpha * x_ref[0] + gate_ref[0] * out

    @pl.when(j == pl.num_programs(2) - 1)
    def _():
        def norm_rows(i, carry):
            rs = pl.ds(pl.multiple_of(i * LN_ROWS, LN_ROWS), LN_ROWS)
            r = o_ref[0, rs, :]
            mu = jnp.mean(r, axis=-1, keepdims=True)
            rc = r - mu
            var = jnp.mean(rc * rc, axis=-1, keepdims=True)
            o_ref[0, rs, :] = rc * lax.rsqrt(var + LN_EPS) * lng_ref[...] + lnb_ref[...]
            return carry

        lax.fori_loop(0, o_ref.shape[1] // LN_ROWS, norm_rows, 0)


def _out(m, w_out, x, gate, ln_g, ln_b, alpha):
    bsz, t, d = x.shape
    tm = _tile(t, 512)
    tn = _tile(d, 1024)
    assert bsz * (t // tm) * (d // tn) >= OUT_W_SLOTS - 1
    return pl.pallas_call(
        functools.partial(_out_kernel, alpha=alpha, tn=tn),
        out_shape=jax.ShapeDtypeStruct((bsz, t, d), F32),
        grid=(bsz, t // tm, d // tn),
        in_specs=[pl.BlockSpec((1, tm, d), lambda b, i, j: (b, i, 0)),
                  pl.BlockSpec(memory_space=pl.ANY),
                  pl.BlockSpec((1, tm, tn), lambda b, i, j: (b, i, j)),
                  pl.BlockSpec((1, 1, tn), lambda b, i, j: (b, 0, j)),
                  pl.BlockSpec((1, d), lambda b, i, j: (0, 0)),
                  pl.BlockSpec((1, d), lambda b, i, j: (0, 0))],
        out_specs=pl.BlockSpec((1, tm, d), lambda b, i, j: (b, i, 0)),
        scratch_shapes=[pltpu.VMEM((OUT_W_SLOTS, d, tn), BF16), pltpu.SemaphoreType.DMA((OUT_W_SLOTS,))],
        compiler_params=_params(("arbitrary", "arbitrary", "arbitrary"), OUT_VMEM_LIMIT),
        name="out",
    )(m, w_out, x, gate, ln_g, ln_b)


def _layer(x, ctx, cond, w_ada, b_ada, w_in, rpb, lb_f, lb_b, norm_w, w_pa, w_pb, w_out, ln_g, ln_b, alpha):
    bsz, t, d = x.shape
    nh = (d // 2) // HEAD
    grp = d // 2
    tn = _tile(grp, 1024)
    per = grp // tn

    mod = _adaln(cond, w_ada, b_ada.reshape(1, -1))
    shift, scale, gate = (mod[:3, k * d:(k + 1) * d].reshape(3, 1, d) for k in range(3))
    ctx_row = bsz

    h_lat = _ln_mod(x, shift, scale, lambda b: b)
    h_ctx = _ln_mod(ctx, shift, scale, lambda b: ctx_row)

    first, w_in16 = _proj_first(h_lat, w_in)
    proj = _proj_main(h_lat, w_in16, first, tn)
    proj_ctx = _proj(h_ctx.reshape(1, -1, d), w_in16, 5 * per,
                     lambda j: jnp.where(j < 2 * per, j + per, j + 3 * per), tn, "proj_ctx")
    proj_ctx = proj_ctx.reshape(5 * nh, bsz, ctx.shape[1], HEAD)

    y_a, w_pa16, w_pb16 = _na(proj, proj_ctx, rpb, nh, (w_pa, w_pb))
    y_b, w_out16 = _hgrn2(proj, proj_ctx, lb_f, lb_b, norm_w.reshape(1, HEAD), nh, (w_out,))
    m = _merge(y_a, y_b, w_pa16, w_pb16, proj, nh)
    return _out(m, w_out16, x, gate, ln_g.reshape(1, d), ln_b.reshape(1, d), alpha)


def kernel(x, c, ctx, c_ctx, w_ada, b_ada, w_in, na_rpb, hg_lb_fwd, hg_lb_bwd, hg_norm_w, w_pa, w_pb, w_out,
           ln_g, ln_b):
    depth = w_ada.shape[0]
    assert depth == 1, "the context stream update of deeper stacks is not implemented"
    bsz, _, d = x.shape
    alpha = (2.0 * depth) ** 0.25
    cond = jnp.concatenate([c, c_ctx[None], jnp.zeros((8 - bsz - 1, d), c.dtype)], axis=0)
    return _layer(x, ctx, cond, w_ada[0], b_ada[0], w_in[0], na_rpb[0], hg_lb_fwd, hg_lb_bwd, hg_norm_w[0],
                  w_pa[0], w_pb[0], w_out[0], ln_g[0], ln_b[0], alpha)
```

```python
import functools
import math

import jax
import jax.numpy as jnp
from jax import lax
from jax.experimental import pallas as pl
from jax.experimental.pallas import tpu as pltpu

F32 = jnp.float32
BF16 = jnp.bfloat16

GRID_W = 64
HEAD = 128
NA_KH = 8
NA_KW = 16
NA_QROWS = 4
NA_KROWS = 12
NA_UNROLL = 4
CHUNK = 64
HG_CHUNKS = 16
HG_TILE = HG_CHUNKS * CHUNK
LN_ROWS = 64
OUT_W_SLOTS = 3
LOG2E = math.log2(math.e)
LN_EPS = 1e-6
RMS_EPS = 1e-6
VMEM_LIMIT = 56 * 1024 * 1024
OUT_VMEM_LIMIT = 58 * 1024 * 1024


def _params(semantics, vmem=VMEM_LIMIT):
    return pltpu.CompilerParams(dimension_semantics=semantics, vmem_limit_bytes=vmem)


def _tile(n, pref):
    t = min(n, pref)
    assert n % t == 0, (n, t)
    return t


def _adaln_kernel(c_ref, w_ref, b_ref, o_ref):
    c = c_ref[...]
    s = c * jax.nn.sigmoid(c)
    s_hi = s.astype(BF16)
    s_lo = (s - s_hi.astype(F32)).astype(BF16)
    w = w_ref[...]
    w_hi = w.astype(BF16)
    w_lo = (w - w_hi.astype(F32)).astype(BF16)
    n = s.shape[0]
    lhs = jnp.concatenate([s_hi, s_lo], axis=0)
    a = jnp.dot(lhs, w_hi, preferred_element_type=F32)
    b = jnp.dot(s_hi, w_lo, preferred_element_type=F32)
    o_ref[...] = a[:n] + a[n:] + b + b_ref[...]


def _adaln(cond, w, b):
    d, n = w.shape
    tn = _tile(n, 512)
    return pl.pallas_call(
        _adaln_kernel,
        out_shape=jax.ShapeDtypeStruct((cond.shape[0], n), F32),
        grid=(n // tn,),
        in_specs=[pl.BlockSpec(cond.shape, lambda j: (0, 0)),
                  pl.BlockSpec((d, tn), lambda j: (0, j)),
                  pl.BlockSpec((1, tn), lambda j: (0, j))],
        out_specs=pl.BlockSpec((cond.shape[0], tn), lambda j: (0, j)),
        compiler_params=_params(("arbitrary",)),
        name="adaln",
    )(cond, w, b)


def _ln_mod_kernel(x_ref, shift_ref, scale_ref, o_ref):
    x = x_ref[0]
    mu = jnp.mean(x, axis=-1, keepdims=True)
    xc = x - mu
    var = jnp.mean(xc * xc, axis=-1, keepdims=True)
    y = xc * lax.rsqrt(var + LN_EPS)
    o_ref[0] = (y * (1.0 + scale_ref[0]) + shift_ref[0]).astype(BF16)


def _ln_mod(x, shift, scale, cond_row):
    bsz, t, d = x.shape
    tm = _tile(t, 512)
    cmap = lambda b, i: (cond_row(b), 0, 0)
    return pl.pallas_call(
        _ln_mod_kernel,
        out_shape=jax.ShapeDtypeStruct((bsz, t, d), BF16),
        grid=(bsz, t // tm),
        in_specs=[pl.BlockSpec((1, tm, d), lambda b, i: (b, i, 0)),
                  pl.BlockSpec((1, 1, d), cmap),
                  pl.BlockSpec((1, 1, d), cmap)],
        out_specs=pl.BlockSpec((1, tm, d), lambda b, i: (b, i, 0)),
        compiler_params=_params(("arbitrary", "arbitrary")),
        name="ln_mod",
    )(x, shift, scale)


def _store_head_major(o_ref, acc):
    for s in range(o_ref.shape[1]):
        o_ref[0, s] = acc[:, s * HEAD:(s + 1) * HEAD].astype(BF16)


def _proj_kernel(h_ref, w_ref, o_ref):
    _store_head_major(o_ref, jnp.dot(h_ref[0], w_ref[...], preferred_element_type=F32))


def _cast_specs(weights, n_steps, step):
    specs, shapes = [], []
    for w in weights:
        rows, cols = w.shape
        split = next(s for s in range(n_steps, 0, -1) if rows % s == 0 and (rows // s) % 16 == 0)
        specs.append(pl.BlockSpec((rows // split, cols),
                                  lambda *ids, split=split: (jnp.minimum(step(*ids), split - 1), 0)))
        shapes.append(jax.ShapeDtypeStruct(w.shape, BF16))
    return specs, shapes


def _cast_slabs(srcs, dsts):
    for src, dst in zip(srcs, dsts):
        dst[...] = src[...].astype(BF16)


def _proj_cast_kernel(h_ref, w_hbm, o_ref, w16_ref, w_buf, w_sem):
    step, n_steps, tn = pl.program_id(0), pl.num_programs(0), w_buf.shape[2]

    def tile_copy(s):
        col = s * tn if isinstance(s, int) else pl.multiple_of(s * tn, tn)
        return pltpu.make_async_copy(w_hbm.at[:, pl.ds(col, tn)], w_buf.at[s % OUT_W_SLOTS],
                                     w_sem.at[s % OUT_W_SLOTS])

    @pl.when(step == 0)
    def _():
        for s in range(OUT_W_SLOTS - 1):
            tile_copy(s).start()

    @pl.when(step + OUT_W_SLOTS - 1 < n_steps)
    def _():
        tile_copy(step + OUT_W_SLOTS - 1).start()

    tile_copy(step).wait()
    w16 = w_buf[step % OUT_W_SLOTS].astype(BF16)
    w16_ref[...] = w16
    _store_head_major(o_ref, jnp.dot(h_ref[0], w16, preferred_element_type=F32))


def _proj_first(h, w):
    _, t, d = h.shape
    n = w.shape[1]
    tm = _tile(t, 1024)
    tn = next(c for c in (512, 256, HEAD) if n % c == 0)
    sub = tn // HEAD
    return pl.pallas_call(
        _proj_cast_kernel,
        out_shape=(jax.ShapeDtypeStruct((1, n // HEAD, tm, HEAD), BF16), jax.ShapeDtypeStruct((d, n), BF16)),
        grid=(n // tn,),
        in_specs=[pl.BlockSpec((1, tm, d), lambda j: (0, 0, 0)),
                  pl.BlockSpec(memory_space=pl.ANY)],
        out_specs=[pl.BlockSpec((1, sub, tm, HEAD), lambda j: (0, j, 0, 0)),
                   pl.BlockSpec((d, tn), lambda j: (0, j))],
        scratch_shapes=[pltpu.VMEM((OUT_W_SLOTS, d, tn), F32), pltpu.SemaphoreType.DMA((OUT_W_SLOTS,))],
        compiler_params=_params(("arbitrary",), OUT_VMEM_LIMIT),
        name="proj_first",
    )(h, w)


def _proj(h, w, n_tiles, col_tile, tn, name):
    bsz, t, d = h.shape
    tm = _tile(t, 1024)
    sub = tn // HEAD
    return pl.pallas_call(
        _proj_kernel,
        out_shape=jax.ShapeDtypeStruct((bsz, n_tiles * sub, t, HEAD), BF16),
        grid=(bsz, t // tm, n_tiles),
        in_specs=[pl.BlockSpec((1, tm, d), lambda b, i, j: (b, i, 0)),
                  pl.BlockSpec((d, tn), lambda b, i, j: (0, col_tile(j)))],
        out_specs=pl.BlockSpec((1, sub, tm, HEAD), lambda b, i, j: (b, j, i, 0)),
        compiler_params=_params(("arbitrary", "arbitrary", "arbitrary")),
        name=name,
    )(h, w)


def _proj_main_kernel(h_ref, w_ref, first_ref, o_ref, *, tm):
    rp = pl.program_id(1)

    @pl.when(rp == 0)
    def _():
        o_ref[0, :, 0:tm, :] = first_ref[0]

    acc = jnp.dot(h_ref[0], w_ref[...], preferred_element_type=F32)
    rows = pl.ds(pl.multiple_of(((rp + 1) % 2) * tm, tm), tm)
    for s in range(o_ref.shape[1]):
        o_ref[0, s, rows, :] = acc[:, s * HEAD:(s + 1) * HEAD].astype(BF16)


def _proj_main(h, w, first, tn):
    bsz, t, d = h.shape
    n = w.shape[1]
    tm = _tile(t, 1024)
    nb = t // tm
    sub = tn // HEAD
    assert nb % 2 == 0 and first.shape == (1, n // HEAD, tm, HEAD)
    blk = lambda rp: ((rp + 1) // nb, (rp + 1) % nb)
    return pl.pallas_call(
        functools.partial(_proj_main_kernel, tm=tm),
        out_shape=jax.ShapeDtypeStruct((bsz, n // HEAD, t, HEAD), BF16),
        grid=(n // tn, bsz * nb - 1),
        in_specs=[pl.BlockSpec((1, tm, d), lambda j, rp: (*blk(rp), 0)),
                  pl.BlockSpec((d, tn), lambda j, rp: (0, j)),
                  pl.BlockSpec((1, sub, tm, HEAD), lambda j, rp: (0, j, 0, 0))],
        out_specs=pl.BlockSpec((1, sub, 2 * tm, HEAD), lambda j, rp: (blk(rp)[0], j, blk(rp)[1] // 2, 0)),
        compiler_params=_params(("arbitrary", "arbitrary")),
        name="proj_lat",
    )(h, w, first)


N_DROW = 2 * NA_KH - 1
N_DCOL = 2 * NA_KW - 1


def _na_block_kinds(rows):
    return ((0, 0), (NA_QROWS, 0), (rows - NA_QROWS, rows - NA_KROWS))


def _build_bias(rpb_ref, head, band_s, bias_s, rows):
    shape = (GRID_W, 2 * GRID_W)
    qc = lax.broadcasted_iota(jnp.int32, shape, 0)
    lane = lax.broadcasted_iota(jnp.int32, shape, 1)
    kc = lane & (GRID_W - 1)
    d_col = jnp.clip(kc - qc, -(NA_KW - 1), NA_KW - 1) + (NA_KW - 1)
    col_start = jnp.clip(qc - NA_KW // 2, 0, GRID_W - NA_KW)
    col_off = kc - col_start
    neg = jnp.full(shape, -jnp.inf, F32)
    base = head * (N_DROW * N_DCOL)
    for d in range(N_DROW):
        acc = neg
        for j in range(N_DCOL):
            acc = jnp.where(d_col == j, rpb_ref[base + d * N_DCOL + j] * LOG2E, acc)
        band_s[d] = jnp.where(col_off < 0, neg, jnp.where(col_off < NA_KW, acc, neg))
    left = lane < GRID_W
    for kind, (r0, kb) in enumerate(_na_block_kinds(rows)):
        for qi in range(NA_QROWS):
            r = r0 + qi
            row_start = min(max(r - NA_KH // 2, 0), rows - NA_KH)

            def band(kr):
                return band_s[kr - r + NA_KH - 1] if row_start <= kr < row_start + NA_KH else neg

            for jp in range(NA_KROWS // 2):
                kr = kb + 2 * jp
                bias_s[kind, qi * GRID_W:(qi + 1) * GRID_W, jp * 2 * GRID_W:(jp + 1) * 2 * GRID_W] = (
                    jnp.where(left, band(kr), band(kr + 1)))


def _na_kernel(rpb_ref, q_ref, k_ref, v_ref, z_ref, kc_ref, vc_ref, *refs, rows, n_cast):
    o_ref = refs[n_cast]
    band_s, bias_s = refs[2 * n_cast + 1:]
    _cast_slabs(refs[:n_cast], refs[n_cast + 1:2 * n_cast + 1])
    nq = NA_QROWS * GRID_W
    nk = NA_KROWS * GRID_W
    n_blocks = rows // NA_QROWS
    scale = HEAD ** -0.5 * LOG2E
    kc = kc_ref[0, 0]
    vc = vc_ref[0, 0]
    nt = (((1,), (1,)), ((), ()))

    @pl.when(pl.program_id(1) == 0)
    def _():
        _build_bias(rpb_ref, pl.program_id(0), band_s, bias_s, rows)

    def scores(i):
        r0 = i * NA_QROWS
        kb = jnp.clip(r0 - NA_KH // 2, 0, rows - NA_KROWS)
        kind = jnp.where(i == 0, 0, jnp.where(i == n_blocks - 1, 2, 1))
        q_off = pl.multiple_of(r0 * GRID_W, GRID_W)
        k_off = pl.multiple_of(kb * GRID_W, GRID_W)
        q = (q_ref[0, 0, pl.ds(q_off, nq), :].astype(F32) * scale).astype(BF16)
        s_loc = lax.dot_general(q, k_ref[0, 0, pl.ds(k_off, nk), :], nt, preferred_element_type=F32)
        s_ctx = lax.dot_general(q, kc, nt, preferred_element_type=F32)
        return q_off, k_off, s_loc + bias_s[kind], s_ctx

    def attend(q_off, k_off, s_loc, s_ctx):
        m = jnp.maximum(jnp.max(s_loc, axis=-1, keepdims=True), jnp.max(s_ctx, axis=-1, keepdims=True))
        p_loc = jnp.exp2(s_loc - m)
        p_ctx = jnp.exp2(s_ctx - m)
        denom = jnp.sum(p_loc, axis=-1, keepdims=True) + jnp.sum(p_ctx, axis=-1, keepdims=True)
        o = (jnp.dot(p_loc.astype(BF16), v_ref[0, 0, pl.ds(k_off, nk), :], preferred_element_type=F32)
             + jnp.dot(p_ctx.astype(BF16), vc, preferred_element_type=F32))
        o = o / denom
        z = z_ref[0, 0, pl.ds(q_off, nq), :].astype(F32)
        o_ref[0, pl.ds(q_off, nq), :] = (o * (z * jax.nn.sigmoid(z))).astype(BF16)

    def blocks(n, carry):
        staged = [scores(n * NA_UNROLL + u) for u in range(NA_UNROLL)]
        for args in staged:
            attend(*args)
        return carry

    lax.fori_loop(0, n_blocks // NA_UNROLL, blocks, 0)


def _na(proj, proj_ctx, rpb, nh, cast):
    bsz, _, t, _ = proj.shape
    lc = proj_ctx.shape[2]
    rows = t // GRID_W
    assert rows % (NA_QROWS * NA_UNROLL) == 0 and rows >= NA_KROWS + NA_QROWS
    assert rpb.shape == (nh, N_DROW, N_DCOL)
    lat = lambda g: pl.BlockSpec((1, 1, t, HEAD), lambda h, b: (b, g * nh + h, 0, 0))
    ctx = lambda g: pl.BlockSpec((1, 1, lc, HEAD), lambda h, b: (g * nh + h, b, 0, 0))
    cast_specs, cast_shapes = _cast_specs(cast, nh * bsz, lambda h, b: h * bsz + b)
    return pl.pallas_call(
        functools.partial(_na_kernel, rows=rows, n_cast=len(cast)),
        out_shape=[jax.ShapeDtypeStruct((bsz, t, nh * HEAD), BF16)] + cast_shapes,
        grid=(nh, bsz),
        in_specs=[pl.BlockSpec(memory_space=pltpu.SMEM),
                  lat(0), lat(1), lat(2), lat(3), ctx(0), ctx(1)] + cast_specs,
        out_specs=[pl.BlockSpec((1, t, HEAD), lambda h, b: (b, 0, h))] + cast_specs,
        scratch_shapes=[pltpu.VMEM((N_DROW, GRID_W, 2 * GRID_W), F32),
                        pltpu.VMEM((3, NA_QROWS * GRID_W, NA_KROWS * GRID_W), F32)],
        compiler_params=_params(("arbitrary", "arbitrary")),
        name="na",
    )(rpb.astype(F32).reshape(-1), proj, proj, proj, proj, proj_ctx, proj_ctx, *cast)


def _seg_cumsum(x, reverse):
    r_i = lax.broadcasted_iota(jnp.int32, (CHUNK, 2 * CHUNK), 0)
    c_i = lax.broadcasted_iota(jnp.int32, (CHUNK, 2 * CHUNK), 1) & (CHUNK - 1)
    tri2 = jnp.where((r_i <= c_i) if reverse else (r_i >= c_i), 1.0, 0.0).astype(BF16)
    hi = x.astype(BF16)
    lo = (x - hi.astype(F32)).astype(BF16)
    out = []
    for c in range(x.shape[0] // CHUNK):
        rows = slice(c * CHUNK, (c + 1) * CHUNK)
        out.append(jnp.dot(tri2, jnp.concatenate([hi[rows], lo[rows]], axis=0), preferred_element_type=F32))
    return jnp.concatenate(out, axis=0)


def _forget(f_pre, lb):
    sg = jax.nn.sigmoid(f_pre)
    one_m_lb = 1.0 - lb
    return one_m_lb * (1.0 - sg), jnp.log(lb + one_m_lb * sg)


def _lower_bound(logits):
    m = jnp.max(logits, axis=0, keepdims=True)
    e = jnp.exp(logits - m)
    return e[0:1] / jnp.sum(e, axis=0, keepdims=True)


def _gates(f_pre, lb, reverse):
    kk, logf = _forget(f_pre, lb)
    cum = _seg_cumsum(logf, reverse).reshape(-1, CHUNK, HEAD)
    mid = CHUNK // 2
    ref = cum[:, mid:mid + 1] if reverse else cum[:, mid - 1:mid]
    last = cum[:, 0:1] if reverse else cum[:, CHUNK - 1:CHUNK]
    return kk.reshape(cum.shape), cum, ref, last


def _decay_columns(dec):
    pad = jnp.zeros((HEAD - dec.shape[0], HEAD), F32)
    return jnp.concatenate([dec, pad], axis=0).T


def _ctx_state(f_pre, v, lb, reverse):
    kk, cum, _, last = _gates(f_pre, lb, reverse)
    kl = (kk * jnp.exp(last - cum)).astype(BF16)
    dec_t = _decay_columns(jnp.exp(last).reshape(-1, HEAD))
    tn = (((0,), (0,)), ((), ()))
    n_chunks = kl.shape[0]
    st = jnp.zeros((HEAD, HEAD), F32)
    for c in (reversed(range(n_chunks)) if reverse else range(n_chunks)):
        u = lax.dot_general(kl[c], v[c * CHUNK:(c + 1) * CHUNK], tn, preferred_element_type=F32)
        st = st * dec_t[:, c:c + 1] + u
    return st


def _hg_prepare(qs, f_pre, lb, reverse):
    kk, cum, ref, last = _gates(f_pre, lb, reverse)
    e = jnp.exp(cum - ref)
    qd = qs.reshape(cum.shape) * e
    kd = kk / e
    qe = qd * jnp.exp(ref)
    kl = kd * jnp.exp(last - ref)
    flat = lambda a: a.reshape(-1, HEAD).astype(BF16)
    return flat(qd), flat(kd), flat(qe), flat(kl), jnp.exp(last).reshape(-1, HEAD)


def _hg_kernel(q_ref, ff_ref, fb_ref, i_ref, g_ref, ffc_ref, fbc_ref, ic_ref, lbf_ref, lbb_ref, nw_ref,
               *refs, t, n_cast):
    o_ref = refs[n_cast]
    qs_s, qd_s, kd_s, qe_s, kl_s, dec_s, ob_s = refs[2 * n_cast + 1:]
    _cast_slabs(refs[:n_cast], refs[n_cast + 1:2 * n_cast + 1])
    n_tiles = t // HG_TILE
    lb_f = _lower_bound(lbf_ref[...])
    lb_b = _lower_bound(lbb_ref[...])
    nw = nw_ref[...]
    r_i = lax.broadcasted_iota(jnp.int32, (CHUNK, CHUNK), 0)
    c_i = lax.broadcasted_iota(jnp.int32, (CHUNK, CHUNK), 1)
    nt = (((1,), (1,)), ((), ()))
    tn = (((0,), (0,)), ((), ()))

    def prepare(tix, f_ref, lb, reverse, first):
        rs = pl.ds(pl.multiple_of(tix * HG_TILE, HG_TILE), HG_TILE)
        if first:
            q = q_ref[0, 0, rs, :].astype(F32)
            qs = q * jax.nn.sigmoid(q)
        else:
            qs = qs_s[rs, :]
        return (qs if first else None,) + _hg_prepare(qs, f_ref[0, 0, rs, :].astype(F32), lb, reverse)

    def stage(tix, vals):
        rs = pl.ds(pl.multiple_of(tix * HG_TILE, HG_TILE), HG_TILE)
        qs, qd, kd, qe, kl, dec = vals
        if qs is not None:
            qs_s[rs, :] = qs
        qd_s[rs, :] = qd
        kd_s[rs, :] = kd
        qe_s[rs, :] = qe
        kl_s[rs, :] = kl
        dec_s[tix] = dec

    def scan(tix, st, reverse, emit):
        tri = (r_i <= c_i) if reverse else (r_i >= c_i)
        order = [HG_CHUNKS - 1 - u if reverse else u for u in range(HG_CHUNKS)]
        cs = {j: pl.ds(pl.multiple_of(tix * HG_TILE + j * CHUNK, CHUNK), CHUNK) for j in order}
        v = {j: i_ref[0, 0, cs[j], :] for j in order}
        a = {j: lax.dot_general(qd_s[cs[j], :], kd_s[cs[j], :], nt, preferred_element_type=F32) for j in order}
        u = {j: lax.dot_general(kl_s[cs[j], :], v[j], tn, preferred_element_type=F32) for j in order}
        dec_t = _decay_columns(dec_s[tix])
        dec_b = {j: jnp.broadcast_to(dec_t[:, j:j + 1], (HEAD, HEAD)) for j in order}
        for j in order:
            lhs = jnp.concatenate([qe_s[cs[j], :], jnp.where(tri, a[j], 0.0).astype(BF16)], axis=1)
            rhs = jnp.concatenate([st.astype(BF16), v[j]], axis=0)
            emit(cs[j], jnp.dot(lhs, rhs, preferred_element_type=F32))
            st = st * dec_b[j] + u[j]
        return st

    def direction(f_ref, fc_ref, lb, reverse, first, emit):
        tile_of = (lambda n: n_tiles - 1 - n) if reverse else (lambda n: n)
        stage(tile_of(0), prepare(tile_of(0), f_ref, lb, reverse, first))
        st = _ctx_state(fc_ref[0, 0].astype(F32), ic_ref[0, 0], lb, reverse)

        def trip(n, st):
            nxt = tile_of(jnp.minimum(n + 1, n_tiles - 1))
            vals = prepare(nxt, f_ref, lb, reverse, first)
            st = scan(tile_of(n), st, reverse, emit)
            stage(nxt, vals)
            return st

        lax.fori_loop(0, n_tiles, trip, st)

    def emit_backward(cs, o):
        ob_s[cs, :] = o

    def emit_forward(cs, o):
        o = o + ob_s[cs, :]
        o = o * lax.rsqrt(jnp.mean(o * o, axis=-1, keepdims=True) + RMS_EPS) * nw
        g = g_ref[0, 0, cs, :].astype(F32)
        o_ref[0, cs, :] = (o * (g * jax.nn.sigmoid(g))).astype(BF16)

    direction(fb_ref, fbc_ref, lb_b, True, True, emit_backward)
    direction(ff_ref, ffc_ref, lb_f, False, False, emit_forward)


def _hgrn2(proj, proj_ctx, lb_f, lb_b, norm_w, nh, cast):
    bsz, _, t, _ = proj.shape
    lc = proj_ctx.shape[2]
    n_slots = lb_f.shape[0]
    assert t % HG_TILE == 0 and lc % CHUNK == 0
    lat = lambda g: pl.BlockSpec((1, 1, t, HEAD), lambda b, h: (b, g * nh + h, 0, 0))
    ctx = lambda g: pl.BlockSpec((1, 1, lc, HEAD), lambda b, h: (g * nh + h, b, 0, 0))
    lbs = pl.BlockSpec((n_slots, HEAD), lambda b, h: (0, h))
    staged = pltpu.VMEM((t, HEAD), BF16)
    cast_specs, cast_shapes = _cast_specs(cast, bsz * nh, lambda b, h: b * nh + h)
    return pl.pallas_call(
        functools.partial(_hg_kernel, t=t, n_cast=len(cast)),
        out_shape=[jax.ShapeDtypeStruct((bsz, t, nh * HEAD), BF16)] + cast_shapes,
        grid=(bsz, nh),
        in_specs=[lat(4), lat(5), lat(6), lat(7), lat(8), ctx(2), ctx(3), ctx(4), lbs, lbs,
                  pl.BlockSpec((1, HEAD), lambda b, h: (0, 0))] + cast_specs,
        out_specs=[pl.BlockSpec((1, t, HEAD), lambda b, h: (b, 0, h))] + cast_specs,
        scratch_shapes=[pltpu.VMEM((t, HEAD), F32), staged, staged, staged, staged,
                        pltpu.VMEM((t // HG_TILE, HG_CHUNKS, HEAD), F32), pltpu.VMEM((t, HEAD), F32)],
        compiler_params=_params(("arbitrary", "arbitrary")),
        name="hgrn2",
    )(proj, proj, proj, proj, proj, proj_ctx, proj_ctx, proj_ctx, lb_f, lb_b, norm_w, *cast)


def _merge_kernel(ya_ref, yb_ref, wa_ref, wb_ref, ga_ref, gb_ref, o_ref):
    ta = jnp.dot(ya_ref[0], wa_ref[...], preferred_element_type=F32)
    tb = jnp.dot(yb_ref[0], wb_ref[...], preferred_element_type=F32)
    for s in range(ga_ref.shape[1]):
        cs = slice(s * HEAD, (s + 1) * HEAD)
        m = (jax.nn.sigmoid(ga_ref[0, s].astype(F32)) * ta[:, cs]
             + jax.nn.sigmoid(gb_ref[0, s].astype(F32)) * tb[:, cs])
        o_ref[0, :, cs] = m.astype(BF16)


def _merge(ya, yb, w_pa, w_pb, proj, nh):
    bsz, t, _ = ya.shape
    d = w_pa.shape[1]
    tm = _tile(t, 1024)
    tn = _tile(nh * HEAD, 1024)
    sub = tn // HEAD
    ga0 = 9 * nh // sub
    gb0 = 11 * nh // sub
    assert (9 * nh) % sub == 0 and (11 * nh) % sub == 0
    yspec = pl.BlockSpec((1, tm, nh * HEAD), lambda b, i, j: (b, i, 0))
    wspec = pl.BlockSpec((nh * HEAD, tn), lambda b, i, j: (0, j))
    return pl.pallas_call(
        _merge_kernel,
        out_shape=jax.ShapeDtypeStruct((bsz, t, d), BF16),
        grid=(bsz, t // tm, d // tn),
        in_specs=[yspec, yspec, wspec, wspec,
                  pl.BlockSpec((1, sub, tm, HEAD), lambda b, i, j: (b, ga0 + j, i, 0)),
                  pl.BlockSpec((1, sub, tm, HEAD), lambda b, i, j: (b, gb0 + j, i, 0))],
        out_specs=pl.BlockSpec((1, tm, tn), lambda b, i, j: (b, i, j)),
        compiler_params=_params(("arbitrary", "arbitrary", "arbitrary")),
        name="merge",
    )(ya, yb, w_pa, w_pb, proj, proj)


def _out_kernel(m_ref, w_hbm, x_ref, gate_ref, lng_ref, lnb_ref, o_ref, w_buf, w_sem, *, alpha, tn):
    n_col = pl.num_programs(2)
    j = pl.program_id(2)
    step = (pl.program_id(0) * pl.num_programs(1) + pl.program_id(1)) * n_col + j
    n_steps = pl.num_programs(0) * pl.num_programs(1) * n_col

    def tile_copy(s):
        src = w_hbm.at[:, pl.ds(pl.multiple_of((s % n_col) * tn, tn), tn)]
        return pltpu.make_async_copy(src, w_buf.at[s % OUT_W_SLOTS], w_sem.at[s % OUT_W_SLOTS])

    @pl.when(step == 0)
    def _():
        for s in range(OUT_W_SLOTS - 1):
            tile_copy(s).start()

    @pl.when(step + OUT_W_SLOTS - 1 < n_steps)
    def _():
        tile_copy(step + OUT_W_SLOTS - 1).start()

    tile_copy(step).wait()
    out = jnp.dot(m_ref[0], w_buf[step % OUT_W_SLOTS], preferred_element_type=F32)
    col = pl.multiple_of(j * tn, tn)
    o_ref[0, :, pl.ds(col, tn)] = alpha * x_ref[0] + gate_ref[0] * out

    @pl.when(j == pl.num_programs(2) - 1)
    def _():
        def norm_rows(i, carry):
            rs = pl.ds(pl.multiple_of(i * LN_ROWS, LN_ROWS), LN_ROWS)
            r = o_ref[0, rs, :]
            mu = jnp.mean(r, axis=-1, keepdims=True)
            rc = r - mu
            var = jnp.mean(rc * rc, axis=-1, keepdims=True)
            o_ref[0, rs, :] = rc * lax.rsqrt(var + LN_EPS) * lng_ref[...] + lnb_ref[...]
            return carry

        lax.fori_loop(0, o_ref.shape[1] // LN_ROWS, norm_rows, 0)


def _out(m, w_out, x, gate, ln_g, ln_b, alpha):
    bsz, t, d = x.shape
    tm = _tile(t, 512)
    tn = _tile(d, 1024)
    assert bsz * (t // tm) * (d // tn) >= OUT_W_SLOTS - 1
    return pl.pallas_call(
        functools.partial(_out_kernel, alpha=alpha, tn=tn),
        out_shape=jax.ShapeDtypeStruct((bsz, t, d), F32),
        grid=(bsz, t // tm, d // tn),
        in_specs=[pl.BlockSpec((1, tm, d), lambda b, i, j: (b, i, 0)),
                  pl.BlockSpec(memory_space=pl.ANY),
                  pl.BlockSpec((1, tm, tn), lambda b, i, j: (b, i, j)),
                  pl.BlockSpec((1, 1, tn), lambda b, i, j: (b, 0, j)),
                  pl.BlockSpec((1, d), lambda b, i, j: (0, 0)),
                  pl.BlockSpec((1, d), lambda b, i, j: (0, 0))],
        out_specs=pl.BlockSpec((1, tm, d), lambda b, i, j: (b, i, 0)),
        scratch_shapes=[pltpu.VMEM((OUT_W_SLOTS, d, tn), BF16), pltpu.SemaphoreType.DMA((OUT_W_SLOTS,))],
        compiler_params=_params(("arbitrary", "arbitrary", "arbitrary"), OUT_VMEM_LIMIT),
        name="out",
    )(m, w_out, x, gate, ln_g, ln_b)


def _layer(x, ctx, cond, w_ada, b_ada, w_in, rpb, lb_f, lb_b, norm_w, w_pa, w_pb, w_out, ln_g, ln_b, alpha):
    bsz, t, d = x.shape
    nh = (d // 2) // HEAD
    grp = d // 2
    tn = _tile(grp, 1024)
    per = grp // tn

    mod = _adaln(cond, w_ada, b_ada.reshape(1, -1))
    shift, scale, gate = (mod[:3, k * d:(k + 1) * d].reshape(3, 1, d) for k in range(3))
    ctx_row = bsz

    h_lat = _ln_mod(x, shift, scale, lambda b: b)
    h_ctx = _ln_mod(ctx, shift, scale, lambda b: ctx_row)

    first, w_in16 = _proj_first(h_lat, w_in)
    proj = _proj_main(h_lat, w_in16, first, tn)
    proj_ctx = _proj(h_ctx.reshape(1, -1, d), w_in16, 5 * per,
                     lambda j: jnp.where(j < 2 * per, j + per, j + 3 * per), tn, "proj_ctx")
    proj_ctx = proj_ctx.reshape(5 * nh, bsz, ctx.shape[1], HEAD)

    y_a, w_pa16, w_pb16 = _na(proj, proj_ctx, rpb, nh, (w_pa, w_pb))
    y_b, w_out16 = _hgrn2(proj, proj_ctx, lb_f, lb_b, norm_w.reshape(1, HEAD), nh, (w_out,))
    m = _merge(y_a, y_b, w_pa16, w_pb16, proj, nh)
    return _out(m, w_out16, x, gate, ln_g.reshape(1, d), ln_b.reshape(1, d), alpha)


def kernel(x, c, ctx, c_ctx, w_ada, b_ada, w_in, na_rpb, hg_lb_fwd, hg_lb_bwd, hg_norm_w, w_pa, w_pb, w_out,
           ln_g, ln_b):
    depth = w_ada.shape[0]
    assert depth == 1, "the context stream update of deeper stacks is not implemented"
    bsz, _, d = x.shape
    alpha = (2.0 * depth) ** 0.25
    cond = jnp.concatenate([c, c_ctx[None], jnp.zeros((8 - bsz - 1, d), c.dtype)], axis=0)
    return _layer(x, ctx, cond, w_ada[0], b_ada[0], w_in[0], na_rpb[0], hg_lb_fwd, hg_lb_bwd, hg_norm_w[0],
                  w_pa[0], w_pb[0], w_out[0], ln_g[0], ln_b[0], alpha)
```
